```python
import math
import jax, jax.numpy as jnp
from jax import lax
import numpy as np

D_MODEL = 1024
BATCH = 4
SEQ = 4096
DEPTH = 2
DEC_BATCH = 1
DEC_SEQ = 16384
PAST_LEN = 128

GRID_W = 64
N_MEM = 256
EPS = 1e-6
ROPE_THETA = 10000.0
Q_BLOCK = 128
NEG_BIG = -1e30
GATE_CLIP = 30.0

A_HEADS = 8
A_KV_HEADS = 2
A_HEAD_DIM = 128
A_Q_WIDTH = A_HEADS * A_HEAD_DIM
A_KV_WIDTH = A_KV_HEADS * A_HEAD_DIM

B_PATTERNS = ((128, 1), (512, 4), (2048, 16))
B_GROUPS = 3
B_HEADS_PER_GROUP = 4
B_HEAD_DIM = 64
B_WIDTH = B_GROUPS * B_HEADS_PER_GROUP * B_HEAD_DIM
B_OUT = B_HEADS_PER_GROUP * B_HEAD_DIM

C_HEADS = 8
C_HEAD_DIM = 128
C_WIDTH = C_HEADS * C_HEAD_DIM
C_CHUNK = 64

X_HEADS = 4
X_HEAD_DIM = D_MODEL // X_HEADS

D_FF = 2816
CONV_W = 3

N_BRANCH = 3
IN_WIDTHS = (A_Q_WIDTH, A_KV_WIDTH, A_KV_WIDTH, B_WIDTH, B_WIDTH, B_WIDTH,
             C_WIDTH, C_WIDTH, C_WIDTH, C_WIDTH, C_WIDTH, N_BRANCH * D_MODEL)
IN_WIDTH = A_Q_WIDTH + 2 * A_KV_WIDTH + 3 * B_WIDTH + 5 * C_WIDTH + N_BRANCH * D_MODEL

kernel_name = 'hybrid_gated_parallel_encoder'


def rms_norm(x, g):
    xf = x.astype(jnp.float32)
    y = xf * lax.rsqrt(jnp.mean(xf * xf, axis=-1, keepdims=True) + EPS)
    return (y * g.astype(jnp.float32)).astype(x.dtype)


def rope_tables(pos, dim):
    inv = jnp.power(ROPE_THETA, -(jnp.arange(0, dim, 2, dtype=jnp.float32) / dim))
    ang = pos.astype(jnp.float32)[:, None] * inv[None, :]
    return jnp.cos(ang), jnp.sin(ang)


def apply_rope(x, cos, sin):
    shape = (cos.shape[0],) + (1,) * (x.ndim - 3) + (cos.shape[1],)
    c = cos.reshape(shape)
    s = sin.reshape(shape)
    x1, x2 = jnp.split(x.astype(jnp.float32), 2, axis=-1)
    return jnp.concatenate([x1 * c - x2 * s, x2 * c + x1 * s], axis=-1).astype(x.dtype)


def gqa_axial_attention(aq, ak, av, gq, gk):
    Bsz, S, _ = aq.shape
    q = rms_norm(aq.reshape(Bsz, S, A_HEADS, A_HEAD_DIM), gq)
    k = rms_norm(ak.reshape(Bsz, S, A_KV_HEADS, A_HEAD_DIM), gk)
    v = av.reshape(Bsz, S, A_KV_HEADS, A_HEAD_DIM)
    rows = S // GRID_W
    row = jnp.repeat(jnp.arange(rows), GRID_W)
    col = jnp.tile(jnp.arange(GRID_W), rows)
    half = A_HEAD_DIM // 2
    cr, sr = rope_tables(row, half)
    cc, sc = rope_tables(col, half)

    def axial(t):
        return jnp.concatenate([apply_rope(t[..., :half], cr, sr),
                                apply_rope(t[..., half:], cc, sc)], axis=-1)

    q, k = axial(q), axial(k)
    G = A_HEADS // A_KV_HEADS
    nb = S // Q_BLOCK
    qb = q.reshape(Bsz, nb, Q_BLOCK, A_KV_HEADS, G, A_HEAD_DIM).transpose(1, 0, 2, 3, 4, 5)
    scale = A_HEAD_DIM ** -0.5

    def block(qblk):
        s = jnp.einsum('bqkgd,bskd->bkgqs', qblk, k, preferred_element_type=jnp.float32) * scale
        p = jax.nn.softmax(s, axis=-1).astype(v.dtype)
        return jnp.einsum('bkgqs,bskd->bqkgd', p, v)

    o = lax.map(block, qb)
    return o.transpose(1, 0, 2, 3, 4, 5).reshape(Bsz, S, A_Q_WIDTH)


def dilated_window_attention(bq, bk, bv, gq, gk):
    Bsz, S, _ = bq.shape
    shp = (Bsz, S, B_GROUPS, B_HEADS_PER_GROUP, B_HEAD_DIM)
    cos, sin = rope_tables(jnp.arange(S), B_HEAD_DIM)
    q = apply_rope(rms_norm(bq.reshape(shp), gq[:, None, :]), cos, sin)
    k = apply_rope(rms_norm(bk.reshape(shp), gk[:, None, :]), cos, sin)
    v = bv.reshape(shp)
    nb = S // Q_BLOCK
    t = jnp.arange(S)
    scale = B_HEAD_DIM ** -0.5
    outs, lses = [], []
    for g, (window, dil) in enumerate(B_PATTERNS):
        n_side = window // (2 * dil)
        n_keys = 2 * n_side + 1
        offs = dil * jnp.arange(-n_side, n_side + 1)
        pos = t[:, None] + offs[None, :]
        valid = ((pos >= 0) & (pos < S)).reshape(nb, Q_BLOCK, n_keys)
        idx = jnp.clip(pos, 0, S - 1).reshape(nb, Q_BLOCK, n_keys)
        qg = q[:, :, g].reshape(Bsz, nb, Q_BLOCK, B_HEADS_PER_GROUP, B_HEAD_DIM).swapaxes(0, 1)
        kg = k[:, :, g]
        vg = v[:, :, g]

        def block(args):
            qblk, iblk, mblk = args
            kk = jnp.take(kg, iblk, axis=1)
            vv = jnp.take(vg, iblk, axis=1)
            s = jnp.einsum('bqhd,bqkhd->bhqk', qblk, kk, preferred_element_type=jnp.float32) * scale
            s = jnp.where(mblk[None, None], s, NEG_BIG)
            m = jnp.max(s, axis=-1, keepdims=True)
            e = jnp.where(mblk[None, None], jnp.exp(s - m), 0.0)
            den = jnp.sum(e, axis=-1)
            lse = m[..., 0] + jnp.log(den)
            p = (e / den[..., None]).astype(vv.dtype)
            return jnp.einsum('bhqk,bqkhd->bqhd', p, vv), lse

        o, lse = lax.map(block, (qg, idx, valid))
        outs.append(o.swapaxes(0, 1).reshape(Bsz, S, B_HEADS_PER_GROUP, B_HEAD_DIM))
        lses.append(lse.transpose(1, 0, 3, 2).reshape(Bsz, S, B_HEADS_PER_GROUP))
    w = jax.nn.softmax(jnp.stack(lses, axis=0), axis=0)
    o = jnp.einsum('gbsh,gbshd->bshd', w.astype(v.dtype), jnp.stack(outs, axis=0))
    return o.reshape(Bsz, S, B_OUT)


def layer_lower_bound(raw, layer):
    p = jax.nn.softmax(raw.astype(jnp.float32), axis=0)
    return (jnp.cumsum(p, axis=0) - p[0])[layer]


def forget_gate(z, lb):
    z = jnp.clip(z.astype(jnp.float32), -GATE_CLIP, GATE_CLIP)
    f = lb + (1.0 - lb) * jax.nn.sigmoid(z)
    log_f = jnp.log(f)
    k = (1.0 - lb) * jax.nn.sigmoid(-z)
    return log_f, k


def hgrn2_chunk_scan(q, k, i, log_f):
    Bsz, S, H, Dk = q.shape
    Dv = i.shape[-1]
    nc = S // C_CHUNK

    def to_chunks(a):
        return a.reshape(Bsz, nc, C_CHUNK, H, a.shape[-1]).transpose(1, 0, 3, 2, 4)

    mask = jnp.tril(jnp.ones((C_CHUNK, C_CHUNK), dtype=bool))[:, :, None]

    def step(state, inp):
        qc, kc, ic, lfc = inp
        b = jnp.cumsum(lfc, axis=2)
        diff = b[:, :, :, None, :] - b[:, :, None, :, :]
        decay = jnp.where(mask, jnp.exp(jnp.where(mask, diff, 0.0)), 0.0)
        scores = jnp.einsum('bhtc,bhsc,bhtsc->bhts', qc, kc, decay)
        o = (jnp.einsum('bhts,bhsv->bhtv', scores, ic)
             + jnp.einsum('bhtc,bhcv->bhtv', qc * jnp.exp(b), state))
        b_last = b[:, :, -1:, :]
        state = (jnp.exp(b_last[:, :, 0, :])[..., None] * state
                 + jnp.einsum('bhsc,bhsv->bhcv', kc * jnp.exp(b_last - b), ic))
        return state, o

    state0 = jnp.zeros((Bsz, H, Dk, Dv), jnp.float32)
    _, o = lax.scan(step, state0, (to_chunks(q), to_chunks(k), to_chunks(i), to_chunks(log_f)))
    return o.transpose(1, 0, 3, 2, 4).reshape(Bsz, S, H, Dv)


def hgrn2_bidirectional(cq, ci, cff, cfb, cg, lb_fwd_raw, lb_bwd_raw, gnorm, layer):
    Bsz, S, _ = cq.shape
    hs = (Bsz, S, C_HEADS, C_HEAD_DIM)
    q = jax.nn.silu(cq.astype(jnp.float32)).reshape(hs)
    i = ci.astype(jnp.float32).reshape(hs)
    logf_f, k_f = forget_gate(cff, layer_lower_bound(lb_fwd_raw, layer))
    logf_b, k_b = forget_gate(cfb, layer_lower_bound(lb_bwd_raw, layer))
    o_f = hgrn2_chunk_scan(q, k_f.reshape(hs), i, logf_f.reshape(hs))
    fl = lambda a: jnp.flip(a, axis=1)
    o_b = fl(hgrn2_chunk_scan(fl(q), fl(k_b.reshape(hs)), fl(i), fl(logf_b.reshape(hs))))
    o = rms_norm((o_f + o_b).reshape(Bsz, S, C_WIDTH), gnorm) * jax.nn.silu(cg.astype(jnp.float32))
    return o.astype(cq.dtype)


def parallel_mixer(u, p, layer):
    splits = np.cumsum(np.array(IN_WIDTHS))[:-1].tolist()
    proj = u @ p['w_in'][layer]
    aq, ak, av, bq, bk, bv, cq, ci, cff, cfb, cg, gates = jnp.split(proj, splits, axis=-1)
    y_a = gqa_axial_attention(aq, ak, av, p['a_gq'][layer], p['a_gk'][layer]) @ p['w_br_a'][layer]
    y_b = dilated_window_attention(bq, bk, bv, p['b_gq'][layer], p['b_gk'][layer]) @ p['w_br_b'][layer]
    y_c = hgrn2_bidirectional(cq, ci, cff, cfb, cg, p['c_lb_fwd'], p['c_lb_bwd'],
                              p['c_gnorm'][layer], layer) @ p['w_br_c'][layer]
    g_a, g_b, g_c = jnp.split(jax.nn.sigmoid(gates), N_BRANCH, axis=-1)
    merged = g_a * y_a + g_b * y_b + g_c * y_c
    return merged @ p['w_mix_out'][layer]


def memory_cross_attention(u, mem_n, p, layer):
    Bsz, S, _ = u.shape
    M = mem_n.shape[1]
    q = rms_norm((u @ p['x_wq'][layer]).reshape(Bsz, S, X_HEADS, X_HEAD_DIM), p['x_gq'][layer])
    k, v = jnp.split(mem_n @ p['x_wkv'][layer], 2, axis=-1)
    k = rms_norm(k.reshape(Bsz, M, X_HEADS, X_HEAD_DIM), p['x_gk'][layer])
    v = v.reshape(Bsz, M, X_HEADS, X_HEAD_DIM)
    s = jnp.einsum('bshd,bmhd->bhsm', q, k, preferred_element_type=jnp.float32) * (X_HEAD_DIM ** -0.5)
    pr = jax.nn.softmax(s, axis=-1).astype(v.dtype)
    o = jnp.einsum('bhsm,bmhd->bshd', pr, v).reshape(Bsz, S, D_MODEL)
    return o @ p['x_wo'][layer]


def conv_ffn(u, p, layer):
    h = u @ p['f_wup'][layer]
    c = h.shape[-1]
    rhs = p['f_conv'][layer][:, None, :].astype(h.dtype)
    h = lax.conv_general_dilated(h, rhs, window_strides=(1,),
                                 padding=((CONV_W // 2, CONV_W // 2),),
                                 dimension_numbers=('NWC', 'WIO', 'NWC'),
                                 feature_group_count=c) + p['f_conv_b'][layer]
    a, g = jnp.split(h, 2, axis=-1)
    return (a * jax.nn.silu(g)) @ p['f_wdown'][layer]


def run_trunk(x, mem, p):
    for layer in range(DEPTH):
        x = x + parallel_mixer(rms_norm(x, p['g_mix'][layer]), p, layer)
        x = x + memory_cross_attention(rms_norm(x, p['g_cross'][layer]),
                                       rms_norm(mem, p['g_mem'][layer]), p, layer)
        x = x + conv_ffn(rms_norm(x, p['g_ffn'][layer]), p, layer)
    return x


def setup_inputs(seed: int = 0) -> dict:
    key = jax.random.key(seed)
    ks = iter(jax.random.split(key, 40))

    def nrm(shape, scale):
        return jax.random.normal(next(ks), shape, jnp.float32) * scale

    def gain(shape):
        return 1.0 + nrm(shape, 0.02)

    D = D_MODEL
    return {
        'x_prompt': nrm((BATCH, SEQ, D), 1.0),
        'x_sample': nrm((DEC_BATCH, DEC_SEQ, D), 1.0),
        'mem_prompt': nrm((BATCH, N_MEM, D), 1.0),
        'mem_sample': nrm((DEC_BATCH, N_MEM, D), 1.0),
        'g_mix': gain((DEPTH, D)),
        'w_in': nrm((DEPTH, D, IN_WIDTH), D ** -0.5),
        'a_gq': gain((DEPTH, A_HEAD_DIM)),
        'a_gk': gain((DEPTH, A_HEAD_DIM)),
        'b_gq': gain((DEPTH, B_GROUPS, B_HEAD_DIM)),
        'b_gk': gain((DEPTH, B_GROUPS, B_HEAD_DIM)),
        'c_lb_fwd': nrm((DEPTH, C_WIDTH), 0.5),
        'c_lb_bwd': nrm((DEPTH, C_WIDTH), 0.5),
        'c_gnorm': gain((DEPTH, C_WIDTH)),
        'w_br_a': nrm((DEPTH, A_Q_WIDTH, D), A_Q_WIDTH ** -0.5),
        'w_br_b': nrm((DEPTH, B_OUT, D), B_OUT ** -0.5),
        'w_br_c': nrm((DEPTH, C_WIDTH, D), C_WIDTH ** -0.5),
        'w_mix_out': nrm((DEPTH, D, D), D ** -0.5),
        'g_cross': gain((DEPTH, D)),
        'g_mem': gain((DEPTH, D)),
        'x_wq': nrm((DEPTH, D, D), D ** -0.5),
        'x_wkv': nrm((DEPTH, D, 2 * D), D ** -0.5),
        'x_gq': gain((DEPTH, X_HEAD_DIM)),
        'x_gk': gain((DEPTH, X_HEAD_DIM)),
        'x_wo': nrm((DEPTH, D, D), D ** -0.5),
        'g_ffn': gain((DEPTH, D)),
        'f_wup': nrm((DEPTH, D, 2 * D_FF), D ** -0.5),
        'f_conv': nrm((DEPTH, CONV_W, 2 * D_FF), CONV_W ** -0.5),
        'f_conv_b': nrm((DEPTH, 2 * D_FF), 0.01),
        'f_wdown': nrm((DEPTH, D_FF, D), D_FF ** -0.5),
    }


def reference(x_prompt, x_sample, mem_prompt, mem_sample, g_mix, w_in, a_gq, a_gk, b_gq, b_gk,
              c_lb_fwd, c_lb_bwd, c_gnorm, w_br_a, w_br_b, w_br_c, w_mix_out, g_cross, g_mem,
              x_wq, x_wkv, x_gq, x_gk, x_wo, g_ffn, f_wup, f_conv, f_conv_b, f_wdown):
    p = dict(g_mix=g_mix, w_in=w_in, a_gq=a_gq, a_gk=a_gk, b_gq=b_gq, b_gk=b_gk,
             c_lb_fwd=c_lb_fwd, c_lb_bwd=c_lb_bwd, c_gnorm=c_gnorm, w_br_a=w_br_a, w_br_b=w_br_b,
             w_br_c=w_br_c, w_mix_out=w_mix_out, g_cross=g_cross, g_mem=g_mem, x_wq=x_wq,
             x_wkv=x_wkv, x_gq=x_gq, x_gk=x_gk, x_wo=x_wo, g_ffn=g_ffn, f_wup=f_wup,
             f_conv=f_conv, f_conv_b=f_conv_b, f_wdown=f_wdown)
    y_prompt = run_trunk(x_prompt, mem_prompt, p)
    y_sample = run_trunk(x_sample, mem_sample, p)
    return (y_prompt, y_sample)
```

```python
import functools

import numpy as np
import jax
import jax.numpy as jnp
from jax import lax
from jax.experimental import pallas as pl
from jax.experimental.pallas import tpu as pltpu

F32 = jnp.float32
BF16 = jnp.bfloat16

D_MODEL = 1024
GRID_W = 64
EPS = 1e-6
ROPE_THETA = 10000.0
NEG_BIG = -1e30
GATE_CLIP = 30.0

A_HEADS = 8
A_KV_HEADS = 2
A_HEAD_DIM = 128
A_GROUP = A_HEADS // A_KV_HEADS

B_PATTERNS = ((128, 1), (512, 4), (2048, 16))
B_HEADS = 4
B_HEAD_DIM = 64
B_GROUP_WIDTH = B_HEADS * B_HEAD_DIM
B_SIDE = 64

C_HEADS = 8
C_HEAD_DIM = 128
C_CHUNK = 64

X_HEADS = 4
X_HEAD_DIM = D_MODEL // X_HEADS

D_FF = 2816
FF_CHUNK = 256

COL_GATES = 0
COL_BQ = 3072
COL_BK = 3840
COL_BV = 4608
COL_AK = 5376
COL_AV = 5632
COL_AQ = 6144
COL_CQ = 7168
COL_CI = 8192
COL_CFF = 9216
COL_CFB = 10240
COL_CG = 11264
PROJ_WIDTH = 12288

VMEM_LIMIT_BYTES = 56 * 1024 * 1024


def _cparams(n_axes, vmem=VMEM_LIMIT_BYTES):
    return pltpu.CompilerParams(dimension_semantics=("arbitrary",) * n_axes, vmem_limit_bytes=vmem)


def _dot(a, b):
    return jnp.dot(a, b, preferred_element_type=F32)


def _dot_nt(a, b):
    return lax.dot_general(a, b, (((1,), (1,)), ((), ())), preferred_element_type=F32)


def _dot_tn(a, b):
    return lax.dot_general(a, b, (((0,), (0,)), ((), ())), preferred_element_type=F32)


def _rms(x, g):
    ms = jnp.mean(x * x, axis=-1, keepdims=True)
    return x * lax.rsqrt(ms + EPS) * g


def _sigmoid(x):
    return 1.0 / (1.0 + jnp.exp(-x))


def _seq_table(groups, tile):
    lo, hi, sid = [], [], []
    off, s_idx = 0, 0
    for (b, s) in groups:
        assert s % tile == 0
        for _ in range(b):
            for _ in range(s // tile):
                lo.append(off)
                hi.append(off + s)
                sid.append(s_idx)
            off += s
            s_idx += 1
    return np.asarray(lo, np.int32), np.asarray(hi, np.int32), np.asarray(sid, np.int32)


def _proj_kernel(x_ref, g_ref, w_ref, o_ref, u_sc):
    @pl.when(pl.program_id(1) == 0)
    def _():
        u_sc[...] = _rms(x_ref[...], g_ref[...]).astype(BF16)

    o_ref[...] = _dot(u_sc[...], w_ref[...]).astype(o_ref.dtype)


def _norm_proj(x, g, w, tm=1024, tn=1024):
    t, k = x.shape
    n = w.shape[1]
    return pl.pallas_call(
        _proj_kernel,
        grid=(t // tm, n // tn),
        in_specs=[
            pl.BlockSpec((tm, k), lambda i, j: (i, 0)),
            pl.BlockSpec((1, k), lambda i, j: (0, 0)),
            pl.BlockSpec((k, tn), lambda i, j: (0, j)),
        ],
        out_specs=pl.BlockSpec((tm, tn), lambda i, j: (i, j)),
        out_shape=jax.ShapeDtypeStruct((t, n), BF16),
        scratch_shapes=[pltpu.VMEM((tm, k), BF16)],
        compiler_params=_cparams(2),
    )(x, g, w)


def _rope(x, c, s1, s2):
    return x * c + pltpu.roll(x, 96, 1) * s1 + pltpu.roll(x, 32, 1) * s2


def _prep_kernel(pos_ref, aq_ref, ak_ref, av_ref, bq_ref, bk_ref,
                 ca_ref, s1a_ref, s2a_ref, cb_ref, s1b_ref, s2b_ref,
                 gaq_ref, gak_ref, gbq_ref, gbk_ref, bd_ref,
                 oaq_ref, oak_ref, oav_ref, obq_ref, obk_ref):
    del pos_ref
    ca, s1a, s2a = ca_ref[...], s1a_ref[...], s2a_ref[...]
    cb, s1b, s2b = cb_ref[...], s1b_ref[...], s2b_ref[...]
    a_scale = A_HEAD_DIM ** -0.5
    b_scale = B_HEAD_DIM ** -0.5

    def a_head(x, g, scale):
        y = _rms(x.astype(F32), g)
        return (_rope(y, ca, s1a, s2a) * scale).astype(BF16)

    for h in range(A_HEADS):
        sl = slice(h * 128, (h + 1) * 128)
        oaq_ref[:, sl] = a_head(aq_ref[:, sl], gaq_ref[...], a_scale)
    for h in range(A_KV_HEADS):
        sl = slice(h * 128, (h + 1) * 128)
        oak_ref[:, sl] = a_head(ak_ref[:, sl], gak_ref[...], 1.0)
        oav_ref[:, h * 256:h * 256 + 128] = av_ref[:, sl]
        oav_ref[:, h * 256 + 128:(h + 1) * 256] = jnp.ones((av_ref.shape[0], 128), BF16)

    bd = bd_ref[...]

    def b_group(x_ref, g_ref, o_ref, scale):
        for grp in range(3):
            sl = slice(grp * 256, (grp + 1) * 256)
            x = x_ref[:, sl].astype(F32)
            ms = _dot((x * x).astype(BF16), bd)
            y = x * lax.rsqrt(ms + EPS) * g_ref[:, sl]
            for half in range(2):
                hs = slice(half * 128, (half + 1) * 128)
                o_ref[:, grp * 256 + half * 128:grp * 256 + (half + 1) * 128] = (
                    _rope(y[:, hs], cb, s1b, s2b) * scale).astype(BF16)

    b_group(bq_ref, gbq_ref, obq_ref, b_scale)
    b_group(bk_ref, gbk_ref, obk_ref, 1.0)


def _rope_tables(smax):
    half = A_HEAD_DIM // 2
    inv = jnp.power(ROPE_THETA, -(jnp.arange(0, half, 2, dtype=F32) / half))
    t = jnp.arange(smax)
    lane = jnp.arange(128)
    first = (lane % 64) < 32

    def tables(ang):
        c = jnp.cos(ang)
        s = jnp.sin(ang)
        return c, jnp.where(first[None, :], -s, 0.0), jnp.where(first[None, :], 0.0, s)

    inv128 = jnp.tile(inv, 4)[None, :]
    row = (t // GRID_W).astype(F32)[:, None]
    col = (t % GRID_W).astype(F32)[:, None]
    ang_a = jnp.where((lane < 64)[None, :], row * inv128, col * inv128)
    ang_b = t.astype(F32)[:, None] * inv128
    return tables(ang_a), tables(ang_b)


def _qk_prep(proj, tabs_a, tabs_b, gaq, gak, gbq, gbk, groups, tm=512):
    t = proj.shape[0]
    lo, _, _ = _seq_table(groups, tm)
    pos_blk = jnp.asarray((np.arange(t // tm) * tm - lo) // tm, jnp.int32)
    bd = jnp.asarray(np.kron(np.eye(4), np.full((64, 64), 1.0 / 64)), BF16)

    def col(width, start):
        return pl.BlockSpec((tm, width), lambda i, p: (i, start // width))

    tab = pl.BlockSpec((tm, 128), lambda i, p: (p[i], 0))

    def full(shape):
        return pl.BlockSpec(shape, lambda i, p: (0, 0))

    def out(width):
        return pl.BlockSpec((tm, width), lambda i, p: (i, 0))

    grid_spec = pltpu.PrefetchScalarGridSpec(
        num_scalar_prefetch=1,
        grid=(t // tm,),
        in_specs=[col(1024, COL_AQ), col(256, COL_AK), col(256, COL_AV), col(768, COL_BQ), col(768, COL_BK),
                  tab, tab, tab, tab, tab, tab,
                  full((1, 128)), full((1, 128)), full((1, 768)), full((1, 768)), full((256, 256))],
        out_specs=[out(1024), out(256), out(512), out(768), out(768)],
    )
    return pl.pallas_call(
        _prep_kernel,
        grid_spec=grid_spec,
        out_shape=[jax.ShapeDtypeStruct((t, w), BF16) for w in (1024, 256, 512, 768, 768)],
        compiler_params=_cparams(1),
    )(pos_blk, proj, proj, proj, proj, proj, *tabs_a, *tabs_b, gaq, gak, gbq, gbk, bd)


def _flash_kernel(q_ref, k_ref, v_ref, o_ref, m_sc, acc_sc):
    j = pl.program_id(3)

    @pl.when(j == 0)
    def _():
        m_sc[...] = jnp.full(m_sc.shape, -jnp.inf, F32)
        acc_sc[...] = jnp.zeros(acc_sc.shape, F32)

    k = k_ref[...]
    v = v_ref[...]
    for h in range(A_GROUP):
        q = q_ref[:, h * 128:(h + 1) * 128]
        s = _dot_nt(q, k)
        m_prev = m_sc[h]
        m_new = jnp.maximum(m_prev, jnp.max(s, axis=-1, keepdims=True))
        alpha = jnp.exp(m_prev - m_new)
        p = jnp.exp(s - m_new[:, :1]).astype(BF16)
        pv = _dot(p, v)
        acc_sc[h] = acc_sc[h] * jnp.concatenate([alpha, alpha], axis=1) + pv
        m_sc[h] = m_new

    @pl.when(j == pl.num_programs(3) - 1)
    def _():
        for h in range(A_GROUP):
            acc = acc_sc[h]
            o_ref[:, h * 128:(h + 1) * 128] = (acc[:, :128] / acc[:, 128:]).astype(o_ref.dtype)


def _flash_group(aq, ak, av, o_prev, tok_off, b, s, tq, tk):
    t = aq.shape[0]
    nq, nk = s // tq, s // tk
    qoff, koff = tok_off // tq, tok_off // tk
    in_specs = [
        pl.BlockSpec((tq, 512), lambda bb, g, i, j: (qoff + bb * nq + i, g)),
        pl.BlockSpec((tk, 128), lambda bb, g, i, j: (koff + bb * nk + j, g)),
        pl.BlockSpec((tk, 256), lambda bb, g, i, j: (koff + bb * nk + j, g)),
    ]
    args = [aq, ak, av]
    aliases = {}
    kernel = _flash_kernel
    if o_prev is not None:
        in_specs.append(pl.BlockSpec(memory_space=pl.ANY))
        args.append(o_prev)
        aliases = {3: 0}

        def kernel(q_ref, k_ref, v_ref, prev_ref, o_ref, m_sc, acc_sc):
            del prev_ref
            _flash_kernel(q_ref, k_ref, v_ref, o_ref, m_sc, acc_sc)

    return pl.pallas_call(
        kernel,
        grid=(b, A_KV_HEADS, nq, nk),
        in_specs=in_specs,
        out_specs=pl.BlockSpec((tq, 512), lambda bb, g, i, j: (qoff + bb * nq + i, g)),
        out_shape=jax.ShapeDtypeStruct((t, 1024), BF16),
        scratch_shapes=[pltpu.VMEM((A_GROUP, tq, 128), F32), pltpu.VMEM((A_GROUP, tq, 256), F32)],
        input_output_aliases=aliases,
        compiler_params=_cparams(4),
    )(*args)


def _attention_a(aq, ak, av, groups, tq=512, tk=512):
    out = None
    off = 0
    for (b, s) in groups:
        out = _flash_group(aq, ak, av, out, off, b, s, min(tq, s), min(tk, s))
        off += b * s
    return out


def _band_kernel(lo_ref, hi_ref, q_ref, kp_ref, kc_ref, kn_ref, vp_ref, vc_ref, vn_ref, o_ref, lse_ref, *, tq):
    i = pl.program_id(1)
    q0 = i * tq
    lo = lo_ref[i]
    hi = hi_ref[i]
    q = q_ref[...]
    k = jnp.concatenate([kp_ref[...], kc_ref[...], kn_ref[...]], axis=0)
    v = jnp.concatenate([vp_ref[...], vc_ref[...], vn_ref[...]], axis=0)
    nk = tq + 2 * B_SIDE
    rel = (lax.broadcasted_iota(jnp.int32, (tq, nk), 1) - B_SIDE) - lax.broadcasted_iota(jnp.int32, (tq, nk), 0)
    kpos = q0 - B_SIDE + lax.broadcasted_iota(jnp.int32, (tq, nk), 1)
    mask = (jnp.abs(rel) <= B_SIDE) & (kpos >= lo) & (kpos < hi)
    head_of_lane = lax.broadcasted_iota(jnp.int32, (1, B_GROUP_WIDTH), 1) // B_HEAD_DIM
    out = jnp.zeros((tq, B_GROUP_WIDTH), F32)
    lse_out = jnp.zeros((tq, B_GROUP_WIDTH), F32)
    for h in range(B_HEADS):
        hm = head_of_lane == h
        qh = jnp.where(hm, q, jnp.zeros_like(q))
        s = jnp.where(mask, _dot_nt(qh, k), NEG_BIG)
        m = jnp.max(s, axis=-1, keepdims=True)
        e = jnp.where(mask, jnp.exp(s - m), 0.0)
        den = jnp.sum(e, axis=-1, keepdims=True)
        p = (e / den).astype(BF16)
        out = jnp.where(hm, _dot(p, v), out)
        lse_out = jnp.where(hm, m + jnp.log(den), lse_out)
    o_ref[...] = out.astype(o_ref.dtype)
    lse_ref[...] = lse_out


def _band_attention(bq, bk, proj, grp, dil, groups, tq=128):
    t = bq.shape[0]
    rows = t // dil
    nq = rows // tq
    lo, hi, _ = _seq_table([(b, s // dil) for (b, s) in groups], tq)
    q_view = bq.reshape(rows, dil * 768)
    k_view = bk.reshape(rows, dil * 768)
    v_view = proj.reshape(rows, dil * PROJ_WIDTH)
    r64 = tq // B_SIDE
    n64 = rows // B_SIDE
    vblk = COL_BV // 256 + grp
    pw = PROJ_WIDTH // 256

    def qk(shape_rows, row_fn):
        return pl.BlockSpec((shape_rows, 256), lambda r, i, lo_, hi_: (row_fn(i), r * 3 + grp))

    def vv(shape_rows, row_fn):
        return pl.BlockSpec((shape_rows, 256), lambda r, i, lo_, hi_: (row_fn(i), r * pw + vblk))

    prev_fn = lambda i: jnp.maximum(i * r64 - 1, 0)
    next_fn = lambda i: jnp.minimum((i + 1) * r64, n64 - 1)
    cur_fn = lambda i: i
    grid_spec = pltpu.PrefetchScalarGridSpec(
        num_scalar_prefetch=2,
        grid=(dil, nq),
        in_specs=[qk(tq, cur_fn), qk(B_SIDE, prev_fn), qk(tq, cur_fn), qk(B_SIDE, next_fn),
                  vv(B_SIDE, prev_fn), vv(tq, cur_fn), vv(B_SIDE, next_fn)],
        out_specs=[pl.BlockSpec((tq, 256), lambda r, i, lo_, hi_: (i, r)),
                   pl.BlockSpec((tq, 256), lambda r, i, lo_, hi_: (i, r))],
    )
    o, lse = pl.pallas_call(
        functools.partial(_band_kernel, tq=tq),
        grid_spec=grid_spec,
        out_shape=[jax.ShapeDtypeStruct((rows, dil * 256), BF16), jax.ShapeDtypeStruct((rows, dil * 256), F32)],
        compiler_params=_cparams(2),
    )(jnp.asarray(lo), jnp.asarray(hi), q_view, k_view, k_view, k_view, v_view, v_view, v_view)
    return o.reshape(t, 256), lse.reshape(t, 256)


def _segment_products(f, row):
    levels = []
    p = f
    x = jnp.ones_like(f)
    for n in (1, 2, 4):
        levels.append((n, p, x))
        n2 = 2 * n
        pos = row & (n2 - 1)
        right = pos >= n
        g = None
        hh = None
        for j in range(n):
            cand = pltpu.roll(p, j + 1, 0)
            g = cand if j == 0 else jnp.where(pos == n + j, cand, g)
            cand = pltpu.roll(p, C_CHUNK - (n2 - 1 - j), 0)
            hh = cand if j == 0 else jnp.where(pos == j, cand, hh)
        p, x = jnp.where(right, p * g, p), jnp.where(right, x, x * hh)
    for n in (8, 16, 32):
        levels.append((n, p, x))
        n2 = 2 * n
        pp, xp = [], []
        for s0 in range(0, C_CHUNK, n2):
            tot_l = p[s0 + n - 1:s0 + n]
            tot_r = p[s0 + n2 - 1:s0 + n2]
            pp += [p[s0:s0 + n], p[s0 + n:s0 + n2] * tot_l]
            xp += [x[s0:s0 + n] * tot_r, x[s0 + n:s0 + n2]]
        p = jnp.concatenate(pp, axis=0)
        x = jnp.concatenate(xp, axis=0)
    return levels, p, x


def _hgrn_chunk(z, cq, ci, lb, st):
    row = lax.broadcasted_iota(jnp.int32, (C_CHUNK, C_HEAD_DIM), 0)
    tt = lax.broadcasted_iota(jnp.int32, (C_CHUNK, C_CHUNK), 0)
    ss = lax.broadcasted_iota(jnp.int32, (C_CHUNK, C_CHUNK), 1)
    zc = jnp.clip(z.astype(F32), -GATE_CLIP, GATE_CLIP)
    e = jnp.exp(-zc)
    sg = 1.0 / (1.0 + e)
    f = lb + (1.0 - lb) * sg
    k = (1.0 - lb) * (e * sg)
    cqf = cq.astype(F32)
    q = cqf * _sigmoid(cqf)
    levels, p64, x64 = _segment_products(f, row)
    scores = jnp.zeros((C_CHUNK, C_CHUNK), F32)
    for (n, pn, xn) in levels:
        n2 = 2 * n
        upper = (row & (n2 - 1)) >= n
        ql = jnp.where(upper, q * pn, 0.0).astype(BF16)
        kl = jnp.where(upper, 0.0, k * xn).astype(BF16)
        shift = n2.bit_length() - 1
        same = (tt >> shift) == (ss >> shift)
        scores = scores + jnp.where(same, _dot_nt(ql, kl), 0.0)
    diag = jnp.sum(q * k, axis=-1, keepdims=True)
    o = (_dot(scores.astype(BF16), ci) + diag * ci.astype(F32)
         + _dot_nt((q * p64).astype(BF16), st.astype(BF16)))
    st_new = st * p64[C_CHUNK - 1:C_CHUNK] + _dot_tn(ci, (k * x64).astype(BF16))
    return o, st_new


def _hgrn_kernel(reset_ref, zf_ref, qf_ref, if_ref, zb_ref, qb_ref, ib_ref, lbf_ref, lbb_ref,
                 of_ref, ob_ref, st_sc, *, n_chunks):
    jblk = pl.program_id(1)

    @pl.when(reset_ref[jblk] == 1)
    def _():
        st_sc[...] = jnp.zeros(st_sc.shape, F32)

    lbf = lbf_ref[...]
    lbb = lbb_ref[...]

    def body(c, carry):
        sl = pl.ds(pl.multiple_of(c * C_CHUNK, C_CHUNK), C_CHUNK)
        o_f, st_f = _hgrn_chunk(zf_ref[sl, :], qf_ref[sl, :], if_ref[sl, :], lbf, st_sc[0])
        o_b, st_b = _hgrn_chunk(zb_ref[sl, :], qb_ref[sl, :], ib_ref[sl, :], lbb, st_sc[1])
        of_ref[sl, :] = o_f.astype(of_ref.dtype)
        ob_ref[sl, :] = o_b.astype(ob_ref.dtype)
        st_sc[0] = st_f
        st_sc[1] = st_b
        return carry

    lax.fori_loop(0, n_chunks, body, 0)


def _hgrn(proj, zb, qb, ib, lbf, lbb, groups, blk=512):
    t = proj.shape[0]
    blk = min(blk, min(s for _, s in groups))
    lo, _, _ = _seq_table(groups, blk)
    reset = jnp.asarray((np.arange(t // blk) * blk == lo).astype(np.int32))

    def pcol(start):
        return pl.BlockSpec((blk, 128), lambda h, j, r: (j, start // 128 + h))

    flat = pl.BlockSpec((blk, 128), lambda h, j, r: (j, h))
    lbs = pl.BlockSpec((1, 128), lambda h, j, r: (0, h))
    grid_spec = pltpu.PrefetchScalarGridSpec(
        num_scalar_prefetch=1,
        grid=(C_HEADS, t // blk),
        in_specs=[pcol(COL_CFF), pcol(COL_CQ), pcol(COL_CI), flat, flat, flat, lbs, lbs],
        out_specs=[flat, flat],
        scratch_shapes=[pltpu.VMEM((2, C_HEAD_DIM, C_HEAD_DIM), F32)],
    )
    return pl.pallas_call(
        functools.partial(_hgrn_kernel, n_chunks=blk // C_CHUNK),
        grid_spec=grid_spec,
        out_shape=[jax.ShapeDtypeStruct((t, 1024), BF16), jax.ShapeDtypeStruct((t, 1024), BF16)],
        compiler_params=_cparams(2),
    )(reset, proj, proj, proj, zb, qb, ib, lbf, lbb)


def _merge_kernel(x_ref, oa_ref, ob0_ref, ob1_ref, ob2_ref, l0_ref, l1_ref, l2_ref, ocf_ref, ocb_ref,
                  cg_ref, gates_ref, gn_ref, wa_ref, wb_ref, wc_ref, wm_ref, o_ref):
    l0, l1, l2 = l0_ref[...], l1_ref[...], l2_ref[...]
    m = jnp.maximum(jnp.maximum(l0, l1), l2)
    w0, w1, w2 = jnp.exp(l0 - m), jnp.exp(l1 - m), jnp.exp(l2 - m)
    ob = (w0 * ob0_ref[...].astype(F32) + w1 * ob1_ref[...].astype(F32)
          + w2 * ob2_ref[...].astype(F32)) / (w0 + w1 + w2)
    oc = ocf_ref[...].astype(F32) + ocb_ref[...].astype(F32)
    cg = cg_ref[...].astype(F32)
    ocn = _rms(oc, gn_ref[...]) * (cg * _sigmoid(cg))
    ya = _dot(oa_ref[...], wa_ref[...])
    yb = _dot(ob.astype(BF16), wb_ref[...])
    yc = _dot(ocn.astype(BF16), wc_ref[...])
    gates = _sigmoid(gates_ref[...].astype(F32))
    merged = gates[:, :1024] * ya + gates[:, 1024:2048] * yb + gates[:, 2048:] * yc
    o_ref[...] = x_ref[...] + _dot(merged.astype(BF16), wm_ref[...])


def _merge(x, oa, obs, lses, ocf, ocb, proj, gn, wa, wb, wc, wm, tm=256):
    t = x.shape[0]

    def rows(width):
        return pl.BlockSpec((tm, width), lambda i: (i, 0))

    def full(shape):
        return pl.BlockSpec(shape, lambda i: (0, 0))

    return pl.pallas_call(
        _merge_kernel,
        grid=(t // tm,),
        in_specs=[rows(1024), rows(1024), rows(256), rows(256), rows(256), rows(256), rows(256), rows(256),
                  rows(1024), rows(1024),
                  pl.BlockSpec((tm, 1024), lambda i: (i, COL_CG // 1024)),
                  pl.BlockSpec((tm, 3072), lambda i: (i, COL_GATES // 3072)),
                  full((1, 1024)), full((1024, 1024)), full((256, 1024)), full((1024, 1024)), full((1024, 1024))],
        out_specs=rows(1024),
        out_shape=jax.ShapeDtypeStruct((t, 1024), F32),
        compiler_params=_cparams(1),
    )(x, oa, *obs, *lses, ocf, ocb, proj, proj, gn, wa, wb, wc, wm)


def _mem_kv_kernel(mem_ref, g_ref, w_ref, gk_ref, k_ref, v_ref):
    u = _rms(mem_ref[0], g_ref[...]).astype(BF16)
    kv = _dot(u, w_ref[...])
    for h in range(X_HEADS):
        sl = slice(h * X_HEAD_DIM, (h + 1) * X_HEAD_DIM)
        k_ref[0, :, sl] = _rms(kv[:, sl], gk_ref[...]).astype(BF16)
    v_ref[0] = kv[:, D_MODEL:].astype(BF16)


def _mem_kv(mem, g, wkv, gk):
    nb, m, d = mem.shape
    return pl.pallas_call(
        _mem_kv_kernel,
        grid=(nb,),
        in_specs=[pl.BlockSpec((1, m, d), lambda b: (b, 0, 0)),
                  pl.BlockSpec((1, d), lambda b: (0, 0)),
                  pl.BlockSpec((d, 2 * d), lambda b: (0, 0)),
                  pl.BlockSpec((1, X_HEAD_DIM), lambda b: (0, 0))],
        out_specs=[pl.BlockSpec((1, m, d), lambda b: (b, 0, 0)), pl.BlockSpec((1, m, d), lambda b: (b, 0, 0))],
        out_shape=[jax.ShapeDtypeStruct((nb, m, d), BF16), jax.ShapeDtypeStruct((nb, m, d), BF16)],
        compiler_params=_cparams(1),
    )(mem, g, wkv, gk)


def _cross_kernel(sid_ref, x_ref, g_ref, wq_ref, gq_ref, k_ref, v_ref, wo_ref, o_ref):
    del sid_ref
    x = x_ref[...]
    u = _rms(x, g_ref[...]).astype(BF16)
    q = _dot(u, wq_ref[...])
    scale = X_HEAD_DIM ** -0.5
    outs = []
    for h in range(X_HEADS):
        sl = slice(h * X_HEAD_DIM, (h + 1) * X_HEAD_DIM)
        qh = (_rms(q[:, sl], gq_ref[...]) * scale).astype(BF16)
        s = _dot_nt(qh, k_ref[0, :, sl])
        m = jnp.max(s, axis=-1, keepdims=True)
        e = jnp.exp(s - m)
        p = (e / jnp.sum(e, axis=-1, keepdims=True)).astype(BF16)
        outs.append(_dot(p, v_ref[0, :, sl]).astype(BF16))
    o = jnp.concatenate(outs, axis=1)
    o_ref[...] = x + _dot(o, wo_ref[...])


def _cross(x, g, wq, gq, k, v, wo, groups, tm=256):
    t, d = x.shape
    _, _, sid = _seq_table(groups, tm)
    m = k.shape[1]

    def full(shape):
        return pl.BlockSpec(shape, lambda i, s: (0, 0))

    grid_spec = pltpu.PrefetchScalarGridSpec(
        num_scalar_prefetch=1,
        grid=(t // tm,),
        in_specs=[pl.BlockSpec((tm, d), lambda i, s: (i, 0)), full((1, d)), full((d, d)), full((1, X_HEAD_DIM)),
                  pl.BlockSpec((1, m, d), lambda i, s: (s[i], 0, 0)),
                  pl.BlockSpec((1, m, d), lambda i, s: (s[i], 0, 0)),
                  full((d, d))],
        out_specs=pl.BlockSpec((tm, d), lambda i, s: (i, 0)),
    )
    return pl.pallas_call(
        _cross_kernel,
        grid_spec=grid_spec,
        out_shape=jax.ShapeDtypeStruct((t, d), F32),
        compiler_params=_cparams(1),
    )(jnp.asarray(sid), x, g, wq, gq, k, v, wo)


def _ffn_kernel(first_ref, last_ref, x_ref, xp_ref, xn_ref, g_ref, wup_ref, cw_ref, cb_ref, wdn_ref, o_ref, u_sc,
                *, tm):
    i = pl.program_id(0)
    x = x_ref[...]
    g = g_ref[...]
    u_sc[0:tm, :] = _rms(x, g).astype(BF16)
    u_sc[tm:tm + 8, :] = _rms(xp_ref[...], g).astype(BF16)
    u_sc[tm + 8:tm + 16, :] = _rms(xn_ref[...], g).astype(BF16)
    has_prev = (first_ref[i] == 0).astype(F32)
    has_next = (last_ref[i] == 0).astype(F32)
    row = lax.broadcasted_iota(jnp.int32, (tm, FF_CHUNK), 0)
    first = row == 0
    last = row == tm - 1
    u = u_sc[...]

    def conv(col0):
        h = _dot(u, wup_ref[:, col0:col0 + FF_CHUNK])
        hm = h[:tm]
        h_prev = jnp.where(first, h[tm + 7:tm + 8] * has_prev, pltpu.roll(hm, 1, 0))
        h_next = jnp.where(last, h[tm + 8:tm + 9] * has_next, pltpu.roll(hm, tm - 1, 0))
        w = cw_ref[:, col0:col0 + FF_CHUNK]
        return h_prev * w[0:1] + hm * w[1:2] + h_next * w[2:3] + cb_ref[:, col0:col0 + FF_CHUNK]

    acc = x
    for c in range(D_FF // FF_CHUNK):
        a = conv(c * FF_CHUNK)
        gg = conv(D_FF + c * FF_CHUNK)
        act = (a * (gg * _sigmoid(gg))).astype(BF16)
        acc = acc + _dot(act, wdn_ref[c * FF_CHUNK:(c + 1) * FF_CHUNK, :])
    o_ref[...] = acc


def _ffn(x, g, wup, cw, cb, wdn, groups, tm=256):
    t, d = x.shape
    lo, hi, _ = _seq_table(groups, tm)
    start = np.arange(t // tm) * tm
    is_first = jnp.asarray((start == lo).astype(np.int32))
    is_last = jnp.asarray((start + tm == hi).astype(np.int32))
    r8 = tm // 8
    n8 = t // 8

    def full(shape):
        return pl.BlockSpec(shape, lambda i, a, b: (0, 0))

    grid_spec = pltpu.PrefetchScalarGridSpec(
        num_scalar_prefetch=2,
        grid=(t // tm,),
        in_specs=[pl.BlockSpec((tm, d), lambda i, a, b: (i, 0)),
                  pl.BlockSpec((8, d), lambda i, a, b: (jnp.maximum(i * r8 - 1, 0), 0)),
                  pl.BlockSpec((8, d), lambda i, a, b: (jnp.minimum((i + 1) * r8, n8 - 1), 0)),
                  full((1, d)), full((d, 2 * D_FF)), full((3, 2 * D_FF)), full((1, 2 * D_FF)), full((D_FF, d))],
        out_specs=pl.BlockSpec((tm, d), lambda i, a, b: (i, 0)),
        scratch_shapes=[pltpu.VMEM((tm + 16, d), BF16)],
    )
    return pl.pallas_call(
        functools.partial(_ffn_kernel, tm=tm),
        grid_spec=grid_spec,
        out_shape=jax.ShapeDtypeStruct((t, d), F32),
        compiler_params=_cparams(1),
    )(is_first, is_last, x, x, x, g, wup, cw, cb, wdn)


def _permute_w_in(w):
    aq, ak, av = w[:, 0:1024], w[:, 1024:1280], w[:, 1280:1536]
    bq, bk, bv = w[:, 1536:2304], w[:, 2304:3072], w[:, 3072:3840]
    cq, ci, cff, cfb, cg = (w[:, 3840 + 1024 * n:4864 + 1024 * n] for n in range(5))
    gates = w[:, 8960:12032]
    pad = jnp.zeros((w.shape[0], 256), w.dtype)
    return jnp.concatenate([gates, bq, bk, bv, ak, av, pad, aq, cq, ci, cff, cfb, cg], axis=1).astype(BF16)


def _lower_bound(raw, layer):
    p = jax.nn.softmax(raw.astype(F32), axis=0)
    return (jnp.cumsum(p, axis=0) - p[0])[layer][None, :]


def _flip_sequences(a, groups):
    parts, off = [], 0
    for (b, s) in groups:
        parts.append(a[off:off + b * s].reshape(b, s, -1)[:, ::-1].reshape(b * s, -1))
        off += b * s
    return jnp.concatenate(parts, axis=0)


def _trunk(x, mem, groups, p):
    depth = p['g_mix'].shape[0]
    smax = max(s for _, s in groups)
    tabs_a, tabs_b = _rope_tables(smax)
    for layer in range(depth):
        row = lambda name: p[name][layer][None, :].astype(F32)
        wbf = lambda name: p[name][layer].astype(BF16)
        proj = _norm_proj(x, row('g_mix'), _permute_w_in(p['w_in'][layer]))
        gbq = jnp.broadcast_to(p['b_gq'][layer][:, None, :], (3, B_HEADS, B_HEAD_DIM)).reshape(1, 768)
        gbk = jnp.broadcast_to(p['b_gk'][layer][:, None, :], (3, B_HEADS, B_HEAD_DIM)).reshape(1, 768)
        aq, ak, av, bq, bk = _qk_prep(proj, tabs_a, tabs_b, row('a_gq'), row('a_gk'),
                                      gbq.astype(F32), gbk.astype(F32), groups)
        oa = _attention_a(aq, ak, av, groups)
        obs, lses = [], []
        for grp, (_, dil) in enumerate(B_PATTERNS):
            o, lse = _band_attention(bq, bk, proj, grp, dil, groups)
            obs.append(o)
            lses.append(lse)
        zb = _flip_sequences(proj[:, COL_CFB:COL_CFB + 1024], groups)
        qb = _flip_sequences(proj[:, COL_CQ:COL_CQ + 1024], groups)
        ib = _flip_sequences(proj[:, COL_CI:COL_CI + 1024], groups)
        ocf, ocb_rev = _hgrn(proj, zb, qb, ib, _lower_bound(p['c_lb_fwd'], layer),
                             _lower_bound(p['c_lb_bwd'], layer), groups)
        ocb = _flip_sequences(ocb_rev, groups)
        x = _merge(x, oa, obs, lses, ocf, ocb, proj, row('c_gnorm'),
                   wbf('w_br_a'), wbf('w_br_b'), wbf('w_br_c'), wbf('w_mix_out'))
        mk, mv = _mem_kv(mem, row('g_mem'), wbf('x_wkv'), row('x_gk'))
        x = _cross(x, row('g_cross'), wbf('x_wq'), row('x_gq'), mk, mv, wbf('x_wo'), groups)
        x = _ffn(x, row('g_ffn'), wbf('f_wup'), p['f_conv'][layer].astype(F32), row('f_conv_b'),
                 wbf('f_wdown'), groups)
    return x


def kernel(x_prompt, x_sample, mem_prompt, mem_sample, g_mix, w_in, a_gq, a_gk, b_gq, b_gk, c_lb_fwd, c_lb_bwd, c_gnorm, w_br_a, w_br_b, w_br_c, w_mix_out, g_cross, g_mem, x_wq, x_wkv, x_gq, x_gk, x_wo, g_ffn, f_wup, f_conv, f_conv_b, f_wdown):
    p = dict(g_mix=g_mix, w_in=w_in, a_gq=a_gq, a_gk=a_gk, b_gq=b_gq, b_gk=b_gk,
             c_lb_fwd=c_lb_fwd, c_lb_bwd=c_lb_bwd, c_gnorm=c_gnorm, w_br_a=w_br_a, w_br_b=w_br_b,
             w_br_c=w_br_c, w_mix_out=w_mix_out, g_cross=g_cross, g_mem=g_mem, x_wq=x_wq,
             x_wkv=x_wkv, x_gq=x_gq, x_gk=x_gk, x_wo=x_wo, g_ffn=g_ffn, f_wup=f_wup,
             f_conv=f_conv, f_conv_b=f_conv_b, f_wdown=f_wdown)
    groups = [(x_prompt.shape[0], x_prompt.shape[1]), (x_sample.shape[0], x_sample.shape[1])]
    d = x_prompt.shape[-1]
    x = jnp.concatenate([x_prompt.reshape(-1, d), x_sample.reshape(-1, d)], axis=0)
    mem = jnp.concatenate([mem_prompt, mem_sample], axis=0)
    y = _trunk(x, mem, groups, p)
    n_prompt = x_prompt.shape[0] * x_prompt.shape[1]
    return (y[:n_prompt].reshape(x_prompt.shape), y[n_prompt:].reshape(x_sample.shape))
```

```python
import functools

import numpy as np
import jax
import jax.numpy as jnp
from jax import lax
from jax.experimental import pallas as pl
from jax.experimental.pallas import tpu as pltpu

F32 = jnp.float32
BF16 = jnp.bfloat16

D_MODEL = 1024
GRID_W = 64
EPS = 1e-6
ROPE_THETA = 10000.0
NEG_BIG = -1e30
GATE_CLIP = 30.0
LOG2E = 1.4426950408889634
BOUND_SLACK = 1.0 + 2.0 ** -6
ROW_SUM_FLOOR = 2.0 ** -100

A_HEADS = 8
A_KV_HEADS = 2
A_HEAD_DIM = 128
A_GROUP = A_HEADS // A_KV_HEADS

B_PATTERNS = ((128, 1), (512, 4), (2048, 16))
B_HEADS = 4
B_HEAD_DIM = 64
B_GROUP_WIDTH = B_HEADS * B_HEAD_DIM
B_SIDE = 64

C_HEADS = 8
C_HEAD_DIM = 128
C_CHUNK = 64

X_HEADS = 4
X_HEAD_DIM = D_MODEL // X_HEADS

D_FF = 2816
FF_CHUNK = 256

COL_GATES = 0
COL_BQ = 3072
COL_BK = 3840
COL_BV = 4608
COL_AK = 5376
COL_AV = 5632
COL_AQ = 6144
COL_CQ = 7168
COL_CI = 8192
COL_CFF = 9216
COL_CFB = 10240
COL_CG = 11264
PROJ_WIDTH = 12288

VMEM_LIMIT_BYTES = 56 * 1024 * 1024


def _cparams(n_axes, vmem=VMEM_LIMIT_BYTES):
    return pltpu.CompilerParams(dimension_semantics=("arbitrary",) * n_axes, vmem_limit_bytes=vmem)


def _dot(a, b):
    return jnp.dot(a, b, preferred_element_type=F32)


def _dot_nt(a, b):
    return lax.dot_general(a, b, (((1,), (1,)), ((), ())), preferred_element_type=F32)


def _dot_tn(a, b):
    return lax.dot_general(a, b, (((0,), (0,)), ((), ())), preferred_element_type=F32)


def _rms(x, g):
    ms = jnp.mean(x * x, axis=-1, keepdims=True)
    return x * lax.rsqrt(ms + EPS) * g


def _sigmoid(x):
    return 1.0 / (1.0 + jnp.exp(-x))


def _seq_table(groups, tile):
    lo, hi, sid = [], [], []
    off, s_idx = 0, 0
    for (b, s) in groups:
        assert s % tile == 0
        for _ in range(b):
            for _ in range(s // tile):
                lo.append(off)
                hi.append(off + s)
                sid.append(s_idx)
            off += s
            s_idx += 1
    return np.asarray(lo, np.int32), np.asarray(hi, np.int32), np.asarray(sid, np.int32)


def _proj_kernel(x_ref, g_ref, w_ref, o_ref, u_sc):
    @pl.when(pl.program_id(1) == 0)
    def _():
        u_sc[...] = _rms(x_ref[...], g_ref[...]).astype(BF16)

    o_ref[...] = _dot(u_sc[...], w_ref[...]).astype(o_ref.dtype)


def _norm_proj(x, g, w, tm=1024, tn=1024):
    t, k = x.shape
    n = w.shape[1]
    return pl.pallas_call(
        _proj_kernel,
        grid=(t // tm, n // tn),
        in_specs=[
            pl.BlockSpec((tm, k), lambda i, j: (i, 0)),
            pl.BlockSpec((1, k), lambda i, j: (0, 0)),
            pl.BlockSpec((k, tn), lambda i, j: (0, j)),
        ],
        out_specs=pl.BlockSpec((tm, tn), lambda i, j: (i, j)),
        out_shape=jax.ShapeDtypeStruct((t, n), BF16),
        scratch_shapes=[pltpu.VMEM((tm, k), BF16)],
        compiler_params=_cparams(2),
    )(x, g, w)


def _rope(x, c, s1, s2):
    return x * c + pltpu.roll(x, 96, 1) * s1 + pltpu.roll(x, 32, 1) * s2


def _prep_kernel(pos_ref, aq_ref, ak_ref, av_ref, bq_ref, bk_ref,
                 ca_ref, s1a_ref, s2a_ref, cb_ref, s1b_ref, s2b_ref,
                 gaq_ref, gak_ref, gbq_ref, gbk_ref, bd_ref,
                 oaq_ref, oak_ref, oav_ref, obq_ref, obk_ref):
    del pos_ref
    ca, s1a, s2a = ca_ref[...], s1a_ref[...], s2a_ref[...]
    cb, s1b, s2b = cb_ref[...], s1b_ref[...], s2b_ref[...]
    a_scale = A_HEAD_DIM ** -0.5 * LOG2E
    b_scale = B_HEAD_DIM ** -0.5
    tm = aq_ref.shape[0]
    lane0 = lax.broadcasted_iota(jnp.int32, (tm, 128), 1) == 0

    def a_head(x, g, scale):
        y = _rms(x.astype(F32), g)
        return _rope(y, ca, s1a, s2a) * scale

    k_bound = jnp.max(jnp.abs(gak_ref[...]), axis=-1, keepdims=True) * (A_HEAD_DIM ** 0.5 * BOUND_SLACK)
    for h in range(A_HEADS):
        sl = slice(h * 128, (h + 1) * 128)
        q = a_head(aq_ref[:, sl], gaq_ref[...], a_scale)
        bound = jnp.sqrt(jnp.sum(q * q, axis=-1, keepdims=True)) * k_bound
        oaq_ref[h // A_GROUP, h % A_GROUP, :, 0:128] = q.astype(BF16)
        oaq_ref[h // A_GROUP, h % A_GROUP, :, 128:256] = jnp.where(lane0, -bound, 0.0).astype(BF16)
    for h in range(A_KV_HEADS):
        sl = slice(h * 128, (h + 1) * 128)
        oak_ref[:, h * 256:h * 256 + 128] = a_head(ak_ref[:, sl], gak_ref[...], 1.0).astype(BF16)
        oak_ref[:, h * 256 + 128:(h + 1) * 256] = jnp.where(lane0, 1.0, 0.0).astype(BF16)
        oav_ref[:, h * 256:h * 256 + 128] = av_ref[:, sl]
        oav_ref[:, h * 256 + 128:(h + 1) * 256] = jnp.ones((tm, 128), BF16)

    bd = bd_ref[...]

    def b_group(x_ref, g_ref, o_ref, scale):
        for grp in range(3):
            sl = slice(grp * 256, (grp + 1) * 256)
            x = x_ref[:, sl].astype(F32)
            ms = _dot((x * x).astype(BF16), bd)
            y = x * lax.rsqrt(ms + EPS) * g_ref[:, sl]
            for half in range(2):
                hs = slice(half * 128, (half + 1) * 128)
                o_ref[:, grp * 256 + half * 128:grp * 256 + (half + 1) * 128] = (
                    _rope(y[:, hs], cb, s1b, s2b) * scale).astype(BF16)

    b_group(bq_ref, gbq_ref, obq_ref, b_scale)
    b_group(bk_ref, gbk_ref, obk_ref, 1.0)


def _rope_tables(smax):
    half = A_HEAD_DIM // 2
    inv = jnp.power(ROPE_THETA, -(jnp.arange(0, half, 2, dtype=F32) / half))
    t = jnp.arange(smax)
    lane = jnp.arange(128)
    first = (lane % 64) < 32

    def tables(ang):
        c = jnp.cos(ang)
        s = jnp.sin(ang)
        return c, jnp.where(first[None, :], -s, 0.0), jnp.where(first[None, :], 0.0, s)

    inv128 = jnp.tile(inv, 4)[None, :]
    row = (t // GRID_W).astype(F32)[:, None]
    col = (t % GRID_W).astype(F32)[:, None]
    ang_a = jnp.where((lane < 64)[None, :], row * inv128, col * inv128)
    ang_b = t.astype(F32)[:, None] * inv128
    return tables(ang_a), tables(ang_b)


def _qk_prep(proj, tabs_a, tabs_b, gaq, gak, gbq, gbk, groups, tm=512):
    t = proj.shape[0]
    lo, _, _ = _seq_table(groups, tm)
    pos_blk = jnp.asarray((np.arange(t // tm) * tm - lo) // tm, jnp.int32)
    bd = jnp.asarray(np.kron(np.eye(4), np.full((64, 64), 1.0 / 64)), BF16)

    def col(width, start):
        return pl.BlockSpec((tm, width), lambda i, p: (i, start // width))

    tab = pl.BlockSpec((tm, 128), lambda i, p: (p[i], 0))

    def full(shape):
        return pl.BlockSpec(shape, lambda i, p: (0, 0))

    def out(width):
        return pl.BlockSpec((tm, width), lambda i, p: (i, 0))

    grid_spec = pltpu.PrefetchScalarGridSpec(
        num_scalar_prefetch=1,
        grid=(t // tm,),
        in_specs=[col(1024, COL_AQ), col(256, COL_AK), col(256, COL_AV), col(768, COL_BQ), col(768, COL_BK),
                  tab, tab, tab, tab, tab, tab,
                  full((1, 128)), full((1, 128)), full((1, 768)), full((1, 768)), full((256, 256))],
        out_specs=[pl.BlockSpec((A_KV_HEADS, A_GROUP, tm, 256), lambda i, p: (0, 0, i, 0)),
                   out(512), out(512), out(768), out(768)],
    )
    return pl.pallas_call(
        _prep_kernel,
        grid_spec=grid_spec,
        out_shape=[jax.ShapeDtypeStruct((A_KV_HEADS, A_GROUP, t, 256), BF16)]
        + [jax.ShapeDtypeStruct((t, w), BF16) for w in (512, 512, 768, 768)],
        compiler_params=_cparams(1),
    )(pos_blk, proj, proj, proj, proj, proj, *tabs_a, *tabs_b, gaq, gak, gbq, gbk, bd)


def _flash_kernel(q_ref, k_ref, v_ref, o_ref, acc_sc, m_sc, *, tq, tk, nk):
    rows = A_GROUP * tq

    def chunk(c):
        return pl.ds(pl.multiple_of(c * tk, tk), tk)

    def scores(c):
        return _dot_nt(q_ref[...].reshape(rows, 256), k_ref[chunk(c), :])

    def accumulate(c, carry):
        p = jnp.exp2(scores(c)).astype(BF16)
        acc_sc[...] += _dot(p, v_ref[chunk(c), :])
        return carry

    acc_sc[...] = jnp.zeros(acc_sc.shape, F32)
    lax.fori_loop(0, nk, accumulate, 0, unroll=2 if nk % 2 == 0 else 1)

    @pl.when(jnp.min(acc_sc[:, 128:129]) < ROW_SUM_FLOOR)
    def _():
        def row_max(c, carry):
            m_sc[...] = jnp.maximum(m_sc[...], jnp.max(scores(c), axis=-1, keepdims=True))
            return carry

        def accumulate_exact(c, carry):
            p = jnp.exp2(scores(c) - m_sc[:, :1]).astype(BF16)
            acc_sc[...] += _dot(p, v_ref[chunk(c), :])
            return carry

        m_sc[...] = jnp.full(m_sc.shape, -jnp.inf, F32)
        lax.fori_loop(0, nk, row_max, 0)
        acc_sc[...] = jnp.zeros(acc_sc.shape, F32)
        lax.fori_loop(0, nk, accumulate_exact, 0)

    for h in range(A_GROUP):
        acc = acc_sc[h * tq:(h + 1) * tq, :]
        o_ref[:, h * 128:(h + 1) * 128] = (acc[:, :128] / acc[:, 128:129]).astype(o_ref.dtype)


def _flash_group(aq, ak, av, o_prev, tok_off, b, s, tq, tk):
    t = ak.shape[0]
    assert tok_off % s == 0 and s % tq == 0 and s % tk == 0
    nq = s // tq
    qoff, soff = tok_off // tq, tok_off // s
    in_specs = [
        pl.BlockSpec((None, A_GROUP, tq, 256), lambda bb, g, i: (g, 0, qoff + bb * nq + i, 0)),
        pl.BlockSpec((s, 256), lambda bb, g, i: (soff + bb, g)),
        pl.BlockSpec((s, 256), lambda bb, g, i: (soff + bb, g)),
    ]
    args = [aq, ak, av]
    aliases = {}
    kernel = functools.partial(_flash_kernel, tq=tq, tk=tk, nk=s // tk)
    if o_prev is not None:
        in_specs.append(pl.BlockSpec(memory_space=pl.ANY))
        args.append(o_prev)
        aliases = {3: 0}
        inner = kernel

        def kernel(q_ref, k_ref, v_ref, prev_ref, o_ref, acc_sc, m_sc):
            del prev_ref
            inner(q_ref, k_ref, v_ref, o_ref, acc_sc, m_sc)

    return pl.pallas_call(
        kernel,
        grid=(b, A_KV_HEADS, nq),
        in_specs=in_specs,
        out_specs=pl.BlockSpec((tq, 512), lambda bb, g, i: (qoff + bb * nq + i, g)),
        out_shape=jax.ShapeDtypeStruct((t, 1024), BF16),
        scratch_shapes=[pltpu.VMEM((A_GROUP * tq, 256), F32), pltpu.VMEM((A_GROUP * tq, 128), F32)],
        input_output_aliases=aliases,
        compiler_params=_cparams(3),
    )(*args)


def _attention_a(aq, ak, av, groups, tq=256, tk=512):
    out = None
    off = 0
    for (b, s) in groups:
        out = _flash_group(aq, ak, av, out, off, b, s, min(tq, s), min(tk, s))
        off += b * s
    return out


def _band_tile(q, k, v, valid):
    tq = q.shape[0]
    nk = tq + 2 * B_SIDE
    rel = (lax.broadcasted_iota(jnp.int32, (tq, nk), 1) - B_SIDE) - lax.broadcasted_iota(jnp.int32, (tq, nk), 0)
    mask = (jnp.abs(rel) <= B_SIDE) & valid
    head_of_lane = lax.broadcasted_iota(jnp.int32, (1, B_GROUP_WIDTH), 1) // B_HEAD_DIM
    out = jnp.zeros((tq, B_GROUP_WIDTH), F32)
    lse_out = jnp.zeros((tq, B_GROUP_WIDTH), F32)
    for h in range(B_HEADS):
        hm = head_of_lane == h
        qh = jnp.where(hm, q, jnp.zeros_like(q))
        s = jnp.where(mask, _dot_nt(qh, k), NEG_BIG)
        m = jnp.max(s, axis=-1, keepdims=True)
        e = jnp.where(mask, jnp.exp(s - m), 0.0)
        den = jnp.sum(e, axis=-1, keepdims=True)
        p = (e / den).astype(BF16)
        out = jnp.where(hm, _dot(p, v), out)
        lse_out = jnp.where(hm, m + jnp.log(den), lse_out)
    return out, lse_out


def _band_kernel(lo_ref, hi_ref, q_ref, kp_ref, kc_ref, kn_ref, vp_ref, vc_ref, vn_ref, o_ref, lse_ref,
                 *scratch, tq, dil):
    i = pl.program_id(0)
    span = tq * dil
    halo = B_SIDE * dil
    nk = tq + 2 * B_SIDE
    t0 = i * span
    lo = lo_ref[i]
    hi = hi_ref[i]
    col = lax.broadcasted_iota(jnp.int32, (tq, nk), 1)

    if dil == 1:
        k = jnp.concatenate([kp_ref[...], kc_ref[...], kn_ref[...]], axis=0)
        v = jnp.concatenate([vp_ref[...], vc_ref[...], vn_ref[...]], axis=0)
        kpos = t0 - halo + col
        out, lse = _band_tile(q_ref[...], k, v, (kpos >= lo) & (kpos < hi))
        o_ref[...] = out.astype(o_ref.dtype)
        lse_ref[...] = lse
        return

    qn_sc, kn_sc, vn_sc, on_sc, ln_sc = scratch
    def stage(dst, row0, src_ref):
        x = src_ref[...].astype(F32)
        n = x.shape[0]
        dst[0, row0:row0 + n, :] = x[:, :128]
        dst[1, row0:row0 + n, :] = x[:, 128:]

    def pick(src, r, n):
        rows = pl.ds(r, n, stride=dil)
        return jnp.concatenate([src[0, rows, :], src[1, rows, :]], axis=1).astype(BF16)

    def put(dst, r, x):
        rows = pl.ds(r, tq, stride=dil)
        dst[0, rows, :] = x[:, :128]
        dst[1, rows, :] = x[:, 128:]

    stage(qn_sc, 0, q_ref)
    for dst, (p_ref, c_ref, n_ref) in ((kn_sc, (kp_ref, kc_ref, kn_ref)), (vn_sc, (vp_ref, vc_ref, vn_ref))):
        stage(dst, 0, p_ref)
        stage(dst, halo, c_ref)
        stage(dst, halo + span, n_ref)

    def residue(r, carry):
        kpos = t0 - halo + r + dil * col
        out, lse = _band_tile(pick(qn_sc, r, tq), pick(kn_sc, r, nk), pick(vn_sc, r, nk),
                              (kpos >= lo) & (kpos < hi))
        put(on_sc, r, out)
        put(ln_sc, r, lse)
        return carry

    lax.fori_loop(0, dil, residue, 0)
    for half in range(2):
        lanes = slice(half * 128, (half + 1) * 128)
        o_ref[:, lanes] = on_sc[half].astype(o_ref.dtype)
        lse_ref[:, lanes] = ln_sc[half]


def _band_attention(bq, bk, proj, grp, dil, groups, tq=128):
    t = bq.shape[0]
    span = tq * dil
    halo = B_SIDE * dil
    lo, hi, _ = _seq_table(groups, span)
    r_halo = span // halo
    n_halo = t // halo
    vblk = COL_BV // 256 + grp

    def blk(rows, row_fn, colblk):
        return pl.BlockSpec((rows, 256), lambda i, lo_, hi_: (row_fn(i), colblk))

    prev_fn = lambda i: jnp.maximum(i * r_halo - 1, 0)
    next_fn = lambda i: jnp.minimum((i + 1) * r_halo, n_halo - 1)
    cur_fn = lambda i: i
    scratch = []
    if dil > 1:
        scratch = [pltpu.VMEM((2, span, 128), F32), pltpu.VMEM((2, span + 2 * halo, 128), F32),
                   pltpu.VMEM((2, span + 2 * halo, 128), F32), pltpu.VMEM((2, span, 128), F32),
                   pltpu.VMEM((2, span, 128), F32)]
    grid_spec = pltpu.PrefetchScalarGridSpec(
        num_scalar_prefetch=2,
        grid=(t // span,),
        in_specs=[blk(span, cur_fn, grp), blk(halo, prev_fn, grp), blk(span, cur_fn, grp), blk(halo, next_fn, grp),
                  blk(halo, prev_fn, vblk), blk(span, cur_fn, vblk), blk(halo, next_fn, vblk)],
        out_specs=[blk(span, cur_fn, 0), blk(span, cur_fn, 0)],
        scratch_shapes=scratch,
    )
    return pl.pallas_call(
        functools.partial(_band_kernel, tq=tq, dil=dil),
        grid_spec=grid_spec,
        out_shape=[jax.ShapeDtypeStruct((t, 256), BF16), jax.ShapeDtypeStruct((t, 256), F32)],
        compiler_params=_cparams(1),
    )(jnp.asarray(lo), jnp.asarray(hi), bq, bk, bk, bk, proj, proj, proj)


def _segment_products(f, row):
    levels = []
    p = f
    x = jnp.ones_like(f)
    for n in (1, 2, 4):
        levels.append((n, p, x))
        n2 = 2 * n
        pos = row & (n2 - 1)
        right = pos >= n
        g = None
        hh = None
        for j in range(n):
            cand = pltpu.roll(p, j + 1, 0)
            g = cand if j == 0 else jnp.where(pos == n + j, cand, g)
            cand = pltpu.roll(p, C_CHUNK - (n2 - 1 - j), 0)
            hh = cand if j == 0 else jnp.where(pos == j, cand, hh)
        p, x = jnp.where(right, p * g, p), jnp.where(right, x, x * hh)
    for n in (8, 16, 32):
        levels.append((n, p, x))
        n2 = 2 * n
        pp, xp = [], []
        for s0 in range(0, C_CHUNK, n2):
            tot_l = p[s0 + n - 1:s0 + n]
            tot_r = p[s0 + n2 - 1:s0 + n2]
            pp += [p[s0:s0 + n], p[s0 + n:s0 + n2] * tot_l]
            xp += [x[s0:s0 + n] * tot_r, x[s0 + n:s0 + n2]]
        p = jnp.concatenate(pp, axis=0)
        x = jnp.concatenate(xp, axis=0)
    return levels, p, x


def _hgrn_chunk(z, cq, ci, lb, st):
    row = lax.broadcasted_iota(jnp.int32, (C_CHUNK, C_HEAD_DIM), 0)
    tt = lax.broadcasted_iota(jnp.int32, (C_CHUNK, C_CHUNK), 0)
    ss = lax.broadcasted_iota(jnp.int32, (C_CHUNK, C_CHUNK), 1)
    zc = jnp.clip(z.astype(F32), -GATE_CLIP, GATE_CLIP)
    e = jnp.exp(-zc)
    sg = 1.0 / (1.0 + e)
    f = lb + (1.0 - lb) * sg
    k = (1.0 - lb) * (e * sg)
    cqf = cq.astype(F32)
    q = cqf * _sigmoid(cqf)
    levels, p64, x64 = _segment_products(f, row)
    scores = jnp.zeros((C_CHUNK, C_CHUNK), F32)
    for (n, pn, xn) in levels:
        n2 = 2 * n
        upper = (row & (n2 - 1)) >= n
        ql = jnp.where(upper, q * pn, 0.0).astype(BF16)
        kl = jnp.where(upper, 0.0, k * xn).astype(BF16)
        shift = n2.bit_length() - 1
        same = (tt >> shift) == (ss >> shift)
        scores = scores + jnp.where(same, _dot_nt(ql, kl), 0.0)
    diag = jnp.sum(q * k, axis=-1, keepdims=True)
    o = (_dot(scores.astype(BF16), ci) + diag * ci.astype(F32)
         + _dot_nt((q * p64).astype(BF16), st.astype(BF16)))
    st_new = st * p64[C_CHUNK - 1:C_CHUNK] + _dot_tn(ci, (k * x64).astype(BF16))
    return o, st_new


def _hgrn_kernel(reset_ref, rev_ref, zf_ref, qf_ref, if_ref, zb_ref, qb_ref, ib_ref, lbf_ref, lbb_ref,
                 of_ref, ob_ref, st_sc, *, n_chunks, heads):
    del rev_ref
    jblk = pl.program_id(1)

    @pl.when(reset_ref[jblk] == 1)
    def _():
        st_sc[...] = jnp.zeros(st_sc.shape, F32)

    lbf = lbf_ref[...]
    lbb = lbb_ref[...]
    rr = lax.broadcasted_iota(jnp.int32, (C_CHUNK, C_CHUNK), 0)
    cc = lax.broadcasted_iota(jnp.int32, (C_CHUNK, C_CHUNK), 1)
    flip = jnp.where(rr + cc == C_CHUNK - 1, 1.0, 0.0).astype(BF16)

    def body(c, carry):
        slf = pl.ds(pl.multiple_of(c * C_CHUNK, C_CHUNK), C_CHUNK)
        slb = pl.ds(pl.multiple_of((n_chunks - 1 - c) * C_CHUNK, C_CHUNK), C_CHUNK)
        for h in range(heads):
            cols = slice(h * C_HEAD_DIM, (h + 1) * C_HEAD_DIM)
            o_f, st_f = _hgrn_chunk(zf_ref[slf, cols], qf_ref[slf, cols], if_ref[slf, cols],
                                    lbf[:, cols], st_sc[0, h])
            of_ref[slf, cols] = o_f.astype(of_ref.dtype)
            st_sc[0, h] = st_f
            o_b, st_b = _hgrn_chunk(_dot(flip, zb_ref[slb, cols]), _dot(flip, qb_ref[slb, cols]),
                                    _dot(flip, ib_ref[slb, cols]).astype(BF16), lbb[:, cols], st_sc[1, h])
            ob_ref[slb, cols] = _dot(flip, o_b.astype(BF16)).astype(ob_ref.dtype)
            st_sc[1, h] = st_b
        return carry

    lax.fori_loop(0, n_chunks, body, 0)


def _hgrn(proj, lbf, lbb, groups, blk=512, heads=2):
    t = proj.shape[0]
    blk = min(blk, min(s for _, s in groups))
    lo, hi, _ = _seq_table(groups, blk)
    idx = np.arange(t // blk)
    reset = jnp.asarray((idx * blk == lo).astype(np.int32))
    rev = jnp.asarray((lo // blk + hi // blk - 1 - idx).astype(np.int32))
    width = heads * C_HEAD_DIM

    def fwd(start):
        return pl.BlockSpec((blk, width), lambda h, j, rs, rv: (j, start // width + h))

    def bwd(start):
        return pl.BlockSpec((blk, width), lambda h, j, rs, rv: (rv[j], start // width + h))

    lbs = pl.BlockSpec((1, width), lambda h, j, rs, rv: (0, h))
    grid_spec = pltpu.PrefetchScalarGridSpec(
        num_scalar_prefetch=2,
        grid=(C_HEADS // heads, t // blk),
        in_specs=[fwd(COL_CFF), fwd(COL_CQ), fwd(COL_CI), bwd(COL_CFB), bwd(COL_CQ), bwd(COL_CI), lbs, lbs],
        out_specs=[fwd(0), bwd(0)],
        scratch_shapes=[pltpu.VMEM((2, heads, C_HEAD_DIM, C_HEAD_DIM), F32)],
    )
    return pl.pallas_call(
        functools.partial(_hgrn_kernel, n_chunks=blk // C_CHUNK, heads=heads),
        grid_spec=grid_spec,
        out_shape=[jax.ShapeDtypeStruct((t, 1024), BF16), jax.ShapeDtypeStruct((t, 1024), BF16)],
        compiler_params=_cparams(2),
    )(reset, rev, proj, proj, proj, proj, proj, proj, lbf, lbb)


def _merge_kernel(x_ref, oa_ref, ob0_ref, ob1_ref, ob2_ref, l0_ref, l1_ref, l2_ref, ocf_ref, ocb_ref,
                  cg_ref, gates_ref, gn_ref, wa_ref, wb_ref, wc_ref, wm_ref, o_ref):
    l0, l1, l2 = l0_ref[...], l1_ref[...], l2_ref[...]
    m = jnp.maximum(jnp.maximum(l0, l1), l2)
    w0, w1, w2 = jnp.exp(l0 - m), jnp.exp(l1 - m), jnp.exp(l2 - m)
    ob = (w0 * ob0_ref[...].astype(F32) + w1 * ob1_ref[...].astype(F32)
          + w2 * ob2_ref[...].astype(F32)) / (w0 + w1 + w2)
    oc = ocf_ref[...].astype(F32) + ocb_ref[...].astype(F32)
    cg = cg_ref[...].astype(F32)
    ocn = _rms(oc, gn_ref[...]) * (cg * _sigmoid(cg))
    ya = _dot(oa_ref[...], wa_ref[...])
    yb = _dot(ob.astype(BF16), wb_ref[...])
    yc = _dot(ocn.astype(BF16), wc_ref[...])
    gates = _sigmoid(gates_ref[...].astype(F32))
    merged = gates[:, :1024] * ya + gates[:, 1024:2048] * yb + gates[:, 2048:] * yc
    o_ref[...] = x_ref[...] + _dot(merged.astype(BF16), wm_ref[...])


def _merge(x, oa, obs, lses, ocf, ocb, proj, gn, wa, wb, wc, wm, tm=256):
    t = x.shape[0]

    def rows(width):
        return pl.BlockSpec((tm, width), lambda i: (i, 0))

    def full(shape):
        return pl.BlockSpec(shape, lambda i: (0, 0))

    return pl.pallas_call(
        _merge_kernel,
        grid=(t // tm,),
        in_specs=[rows(1024), rows(1024), rows(256), rows(256), rows(256), rows(256), rows(256), rows(256),
                  rows(1024), rows(1024),
                  pl.BlockSpec((tm, 1024), lambda i: (i, COL_CG // 1024)),
                  pl.BlockSpec((tm, 3072), lambda i: (i, COL_GATES // 3072)),
                  full((1, 1024)), full((1024, 1024)), full((256, 1024)), full((1024, 1024)), full((1024, 1024))],
        out_specs=rows(1024),
        out_shape=jax.ShapeDtypeStruct((t, 1024), F32),
        compiler_params=_cparams(1),
    )(x, oa, *obs, *lses, ocf, ocb, proj, proj, gn, wa, wb, wc, wm)


def _mem_kv_kernel(mem_ref, g_ref, w_ref, gk_ref, k_ref, v_ref):
    u = _rms(mem_ref[0], g_ref[...]).astype(BF16)
    kv = _dot(u, w_ref[...])
    for h in range(X_HEADS):
        sl = slice(h * X_HEAD_DIM, (h + 1) * X_HEAD_DIM)
        k_ref[0, :, sl] = _rms(kv[:, sl], gk_ref[...]).astype(BF16)
    v_ref[0] = kv[:, D_MODEL:].astype(BF16)


def _mem_kv(mem, g, wkv, gk):
    nb, m, d = mem.shape
    return pl.pallas_call(
        _mem_kv_kernel,
        grid=(nb,),
        in_specs=[pl.BlockSpec((1, m, d), lambda b: (b, 0, 0)),
                  pl.BlockSpec((1, d), lambda b: (0, 0)),
                  pl.BlockSpec((d, 2 * d), lambda b: (0, 0)),
                  pl.BlockSpec((1, X_HEAD_DIM), lambda b: (0, 0))],
        out_specs=[pl.BlockSpec((1, m, d), lambda b: (b, 0, 0)), pl.BlockSpec((1, m, d), lambda b: (b, 0, 0))],
        out_shape=[jax.ShapeDtypeStruct((nb, m, d), BF16), jax.ShapeDtypeStruct((nb, m, d), BF16)],
        compiler_params=_cparams(1),
    )(mem, g, wkv, gk)


def _cross_kernel(sid_ref, x_ref, g_ref, wq_ref, gq_ref, k_ref, v_ref, wo_ref, o_ref):
    del sid_ref
    x = x_ref[...]
    u = _rms(x, g_ref[...]).astype(BF16)
    q = _dot(u, wq_ref[...])
    scale = X_HEAD_DIM ** -0.5
    outs = []
    for h in range(X_HEADS):
        sl = slice(h * X_HEAD_DIM, (h + 1) * X_HEAD_DIM)
        qh = (_rms(q[:, sl], gq_ref[...]) * scale).astype(BF16)
        s = _dot_nt(qh, k_ref[0, :, sl])
        m = jnp.max(s, axis=-1, keepdims=True)
        e = jnp.exp(s - m)
        p = (e / jnp.sum(e, axis=-1, keepdims=True)).astype(BF16)
        outs.append(_dot(p, v_ref[0, :, sl]).astype(BF16))
    o = jnp.concatenate(outs, axis=1)
    o_ref[...] = x + _dot(o, wo_ref[...])


def _cross(x, g, wq, gq, k, v, wo, groups, tm=256):
    t, d = x.shape
    _, _, sid = _seq_table(groups, tm)
    m = k.shape[1]

    def full(shape):
        return pl.BlockSpec(shape, lambda i, s: (0, 0))

    grid_spec = pltpu.PrefetchScalarGridSpec(
        num_scalar_prefetch=1,
        grid=(t // tm,),
        in_specs=[pl.BlockSpec((tm, d), lambda i, s: (i, 0)), full((1, d)), full((d, d)), full((1, X_HEAD_DIM)),
                  pl.BlockSpec((1, m, d), lambda i, s: (s[i], 0, 0)),
                  pl.BlockSpec((1, m, d), lambda i, s: (s[i], 0, 0)),
                  full((d, d))],
        out_specs=pl.BlockSpec((tm, d), lambda i, s: (i, 0)),
    )
    return pl.pallas_call(
        _cross_kernel,
        grid_spec=grid_spec,
        out_shape=jax.ShapeDtypeStruct((t, d), F32),
        compiler_params=_cparams(1),
    )(jnp.asarray(sid), x, g, wq, gq, k, v, wo)


def _ffn_kernel(first_ref, last_ref, x_ref, xp_ref, xn_ref, g_ref, wup_ref, cw_ref, cb_ref, wdn_ref, o_ref, u_sc,
                *, tm):
    i = pl.program_id(0)
    x = x_ref[...]
    g = g_ref[...]
    u_sc[0:tm, :] = _rms(x, g).astype(BF16)
    u_sc[tm:tm + 8, :] = _rms(xp_ref[...], g).astype(BF16)
    u_sc[tm + 8:tm + 16, :] = _rms(xn_ref[...], g).astype(BF16)
    has_prev = (first_ref[i] == 0).astype(F32)
    has_next = (last_ref[i] == 0).astype(F32)
    row = lax.broadcasted_iota(jnp.int32, (tm, FF_CHUNK), 0)
    first = row == 0
    last = row == tm - 1
    u = u_sc[...]

    def conv(col0):
        h = _dot(u, wup_ref[:, col0:col0 + FF_CHUNK])
        hm = h[:tm]
        h_prev = jnp.where(first, h[tm + 7:tm + 8] * has_prev, pltpu.roll(hm, 1, 0))
        h_next = jnp.where(last, h[tm + 8:tm + 9] * has_next, pltpu.roll(hm, tm - 1, 0))
        w = cw_ref[:, col0:col0 + FF_CHUNK]
        return h_prev * w[0:1] + hm * w[1:2] + h_next * w[2:3] + cb_ref[:, col0:col0 + FF_CHUNK]

    acc = x
    for c in range(D_FF // FF_CHUNK):
        a = conv(c * FF_CHUNK)
        gg = conv(D_FF + c * FF_CHUNK)
        act = (a * (gg * _sigmoid(gg))).astype(BF16)
        acc = acc + _dot(act, wdn_ref[c * FF_CHUNK:(c + 1) * FF_CHUNK, :])
    o_ref[...] = acc


def _ffn(x, g, wup, cw, cb, wdn, groups, tm=256):
    t, d = x.shape
    lo, hi, _ = _seq_table(groups, tm)
    start = np.arange(t // tm) * tm
    is_first = jnp.asarray((start == lo).astype(np.int32))
    is_last = jnp.asarray((start + tm == hi).astype(np.int32))
    r8 = tm // 8
    n8 = t // 8

    def full(shape):
        return pl.BlockSpec(shape, lambda i, a, b: (0, 0))

    grid_spec = pltpu.PrefetchScalarGridSpec(
        num_scalar_prefetch=2,
        grid=(t // tm,),
        in_specs=[pl.BlockSpec((tm, d), lambda i, a, b: (i, 0)),
                  pl.BlockSpec((8, d), lambda i, a, b: (jnp.maximum(i * r8 - 1, 0), 0)),
                  pl.BlockSpec((8, d), lambda i, a, b: (jnp.minimum((i + 1) * r8, n8 - 1), 0)),
                  full((1, d)), full((d, 2 * D_FF)), full((3, 2 * D_FF)), full((1, 2 * D_FF)), full((D_FF, d))],
        out_specs=pl.BlockSpec((tm, d), lambda i, a, b: (i, 0)),
        scratch_shapes=[pltpu.VMEM((tm + 16, d), BF16)],
    )
    return pl.pallas_call(
        functools.partial(_ffn_kernel, tm=tm),
        grid_spec=grid_spec,
        out_shape=jax.ShapeDtypeStruct((t, d), F32),
        compiler_params=_cparams(1),
    )(is_first, is_last, x, x, x, g, wup, cw, cb, wdn)


def _permute_w_in(w):
    aq, ak, av = w[:, 0:1024], w[:, 1024:1280], w[:, 1280:1536]
    bq, bk, bv = w[:, 1536:2304], w[:, 2304:3072], w[:, 3072:3840]
    cq, ci, cff, cfb, cg = (w[:, 3840 + 1024 * n:4864 + 1024 * n] for n in range(5))
    gates = w[:, 8960:12032]
    pad = jnp.zeros((w.shape[0], 256), w.dtype)
    return jnp.concatenate([gates, bq, bk, bv, ak, av, pad, aq, cq, ci, cff, cfb, cg], axis=1).astype(BF16)


def _lower_bound(raw, layer):
    p = jax.nn.softmax(raw.astype(F32), axis=0)
    return (jnp.cumsum(p, axis=0) - p[0])[layer][None, :]


def _trunk(x, mem, groups, p):
    depth = p['g_mix'].shape[0]
    smax = max(s for _, s in groups)
    tabs_a, tabs_b = _rope_tables(smax)
    for layer in range(depth):
        row = lambda name: p[name][layer][None, :].astype(F32)
        wbf = lambda name: p[name][layer].astype(BF16)
        proj = _norm_proj(x, row('g_mix'), _permute_w_in(p['w_in'][layer]))
        gbq = jnp.broadcast_to(p['b_gq'][layer][:, None, :], (3, B_HEADS, B_HEAD_DIM)).reshape(1, 768)
        gbk = jnp.broadcast_to(p['b_gk'][layer][:, None, :], (3, B_HEADS, B_HEAD_DIM)).reshape(1, 768)
        aq, ak, av, bq, bk = _qk_prep(proj, tabs_a, tabs_b, row('a_gq'), row('a_gk'),
                                      gbq.astype(F32), gbk.astype(F32), groups)
        oa = _attention_a(aq, ak, av, groups)
        obs, lses = [], []
        for grp, (_, dil) in enumerate(B_PATTERNS):
            o, lse = _band_attention(bq, bk, proj, grp, dil, groups)
            obs.append(o)
            lses.append(lse)
        ocf, ocb = _hgrn(proj, _lower_bound(p['c_lb_fwd'], layer), _lower_bound(p['c_lb_bwd'], layer), groups)
        x = _merge(x, oa, obs, lses, ocf, ocb, proj, row('c_gnorm'),
                   wbf('w_br_a'), wbf('w_br_b'), wbf('w_br_c'), wbf('w_mix_out'))
        mk, mv = _mem_kv(mem, row('g_mem'), wbf('x_wkv'), row('x_gk'))
        x = _cross(x, row('g_cross'), wbf('x_wq'), row('x_gq'), mk, mv, wbf('x_wo'), groups)
        x = _ffn(x, row('g_ffn'), wbf('f_wup'), p['f_conv'][layer].astype(F32), row('f_conv_b'),
                 wbf('f_wdown'), groups)
    return x


def kernel(x_prompt, x_sample, mem_prompt, mem_sample, g_mix, w_in, a_gq, a_gk, b_gq, b_gk, c_lb_fwd, c_lb_bwd, c_gnorm, w_br_a, w_br_b, w_br_c, w_mix_out, g_cross, g_mem, x_wq, x_wkv, x_gq, x_gk, x_wo, g_ffn, f_wup, f_conv, f_conv_b, f_wdown):
    p = dict(g_mix=g_mix, w_in=w_in, a_gq=a_gq, a_gk=a_gk, b_gq=b_gq, b_gk=b_gk,
             c_lb_fwd=c_lb_fwd, c_lb_bwd=c_lb_bwd, c_gnorm=c_gnorm, w_br_a=w_br_a, w_br_b=w_br_b,
             w_br_c=w_br_c, w_mix_out=w_mix_out, g_cross=g_cross, g_mem=g_mem, x_wq=x_wq,
             x_wkv=x_wkv, x_gq=x_gq, x_gk=x_gk, x_wo=x_wo, g_ffn=g_ffn, f_wup=f_wup,
             f_conv=f_conv, f_conv_b=f_conv_b, f_wdown=f_wdown)
    groups = [(x_prompt.shape[0], x_prompt.shape[1]), (x_sample.shape[0], x_sample.shape[1])]
    d = x_prompt.shape[-1]
    x = jnp.concatenate([x_prompt.reshape(-1, d), x_sample.reshape(-1, d)], axis=0)
    mem = jnp.concatenate([mem_prompt, mem_sample], axis=0)
    y = _trunk(x, mem, groups, p)
    n_prompt = x_prompt.shape[0] * x_prompt.shape[1]
    return (y[:n_prompt].reshape(x_prompt.shape), y[n_prompt:].reshape(x_sample.shape))
```

```python
import functools

import numpy as np
import jax
import jax.numpy as jnp
from jax import lax
from jax.experimental import pallas as pl
from jax.experimental.pallas import tpu as pltpu

F32 = jnp.float32
BF16 = jnp.bfloat16

D_MODEL = 1024
GRID_W = 64
EPS = 1e-6
ROPE_THETA = 10000.0
NEG_BIG = -1e30
GATE_CLIP = 30.0
LOG2E = 1.4426950408889634
BOUND_SLACK = 1.0 + 2.0 ** -6
ROW_SUM_FLOOR = 2.0 ** -100

A_HEADS = 8
A_KV_HEADS = 2
A_HEAD_DIM = 128
A_GROUP = A_HEADS // A_KV_HEADS

B_PATTERNS = ((128, 1), (512, 4), (2048, 16))
B_HEADS = 4
B_HEAD_DIM = 64
B_GROUP_WIDTH = B_HEADS * B_HEAD_DIM
B_SIDE = 64

C_HEADS = 8
C_HEAD_DIM = 128
C_CHUNK = 64

X_HEADS = 4
X_HEAD_DIM = D_MODEL // X_HEADS

D_FF = 2816
FF_CHUNK = 256

COL_GATES = 0
COL_BQ = 3072
COL_BK = 3840
COL_BV = 4608
COL_AK = 5376
COL_AV = 5632
COL_AQ = 6144
COL_CQ = 7168
COL_CI = 8192
COL_CFF = 9216
COL_CFB = 10240
COL_CG = 11264
PROJ_WIDTH = 12288

VMEM_LIMIT_BYTES = 56 * 1024 * 1024


def _cparams(n_axes, vmem=VMEM_LIMIT_BYTES):
    return pltpu.CompilerParams(dimension_semantics=("arbitrary",) * n_axes, vmem_limit_bytes=vmem)


def _dot(a, b):
    return jnp.dot(a, b, preferred_element_type=F32)


def _dot_nt(a, b):
    return lax.dot_general(a, b, (((1,), (1,)), ((), ())), preferred_element_type=F32)


def _dot_tn(a, b):
    return lax.dot_general(a, b, (((0,), (0,)), ((), ())), preferred_element_type=F32)


def _rms(x, g):
    ms = jnp.mean(x * x, axis=-1, keepdims=True)
    return x * lax.rsqrt(ms + EPS) * g


def _sigmoid(x):
    return 1.0 / (1.0 + jnp.exp(-x))


def _seq_table(groups, tile):
    lo, hi, sid = [], [], []
    off, s_idx = 0, 0
    for (b, s) in groups:
        assert s % tile == 0
        for _ in range(b):
            for _ in range(s // tile):
                lo.append(off)
                hi.append(off + s)
                sid.append(s_idx)
            off += s
            s_idx += 1
    return np.asarray(lo, np.int32), np.asarray(hi, np.int32), np.asarray(sid, np.int32)


def _proj_kernel(x_ref, g_ref, w_ref, o_ref, u_sc):
    @pl.when(pl.program_id(1) == 0)
    def _():
        u_sc[...] = _rms(x_ref[...], g_ref[...]).astype(BF16)

    o_ref[...] = _dot(u_sc[...], w_ref[...]).astype(o_ref.dtype)


def _norm_proj(x, g, w, tm=1024, tn=1024):
    t, k = x.shape
    n = w.shape[1]
    return pl.pallas_call(
        _proj_kernel,
        grid=(t // tm, n // tn),
        in_specs=[
            pl.BlockSpec((tm, k), lambda i, j: (i, 0)),
            pl.BlockSpec((1, k), lambda i, j: (0, 0)),
            pl.BlockSpec((k, tn), lambda i, j: (0, j)),
        ],
        out_specs=pl.BlockSpec((tm, tn), lambda i, j: (i, j)),
        out_shape=jax.ShapeDtypeStruct((t, n), BF16),
        scratch_shapes=[pltpu.VMEM((tm, k), BF16)],
        compiler_params=_cparams(2),
    )(x, g, w)


def _rope(x, c, s1, s2):
    return x * c + pltpu.roll(x, 96, 1) * s1 + pltpu.roll(x, 32, 1) * s2


def _prep_kernel(pos_ref, aq_ref, ak_ref, av_ref, bq_ref, bk_ref,
                 ca_ref, s1a_ref, s2a_ref, cb_ref, s1b_ref, s2b_ref,
                 gaq_ref, gak_ref, gbq_ref, gbk_ref, bd_ref,
                 oaq_ref, oak_ref, oav_ref, obq_ref, obk_ref):
    del pos_ref
    ca, s1a, s2a = ca_ref[...], s1a_ref[...], s2a_ref[...]
    cb, s1b, s2b = cb_ref[...], s1b_ref[...], s2b_ref[...]
    a_scale = A_HEAD_DIM ** -0.5 * LOG2E
    b_scale = B_HEAD_DIM ** -0.5
    tm = aq_ref.shape[0]
    lane0 = lax.broadcasted_iota(jnp.int32, (tm, 128), 1) == 0

    def a_head(x, g, scale):
        y = _rms(x.astype(F32), g)
        return _rope(y, ca, s1a, s2a) * scale

    k_bound = jnp.max(jnp.abs(gak_ref[...]), axis=-1, keepdims=True) * (A_HEAD_DIM ** 0.5 * BOUND_SLACK)
    for h in range(A_HEADS):
        sl = slice(h * 128, (h + 1) * 128)
        q = a_head(aq_ref[:, sl], gaq_ref[...], a_scale)
        bound = jnp.sqrt(jnp.sum(q * q, axis=-1, keepdims=True)) * k_bound
        oaq_ref[h // A_GROUP, h % A_GROUP, :, 0:128] = q.astype(BF16)
        oaq_ref[h // A_GROUP, h % A_GROUP, :, 128:256] = jnp.where(lane0, -bound, 0.0).astype(BF16)
    for h in range(A_KV_HEADS):
        sl = slice(h * 128, (h + 1) * 128)
        oak_ref[:, h * 256:h * 256 + 128] = a_head(ak_ref[:, sl], gak_ref[...], 1.0).astype(BF16)
        oak_ref[:, h * 256 + 128:(h + 1) * 256] = jnp.where(lane0, 1.0, 0.0).astype(BF16)
        oav_ref[:, h * 256:h * 256 + 128] = av_ref[:, sl]
        oav_ref[:, h * 256 + 128:(h + 1) * 256] = jnp.ones((tm, 128), BF16)

    bd = bd_ref[...]

    def b_group(x_ref, g_ref, o_ref, scale):
        for grp in range(3):
            sl = slice(grp * 256, (grp + 1) * 256)
            x = x_ref[:, sl].astype(F32)
            ms = _dot((x * x).astype(BF16), bd)
            y = x * lax.rsqrt(ms + EPS) * g_ref[:, sl]
            for half in range(2):
                hs = slice(half * 128, (half + 1) * 128)
                o_ref[:, grp * 256 + half * 128:grp * 256 + (half + 1) * 128] = (
                    _rope(y[:, hs], cb, s1b, s2b) * scale).astype(BF16)

    b_group(bq_ref, gbq_ref, obq_ref, b_scale)
    b_group(bk_ref, gbk_ref, obk_ref, 1.0)


def _rope_tables(smax):
    half = A_HEAD_DIM // 2
    inv = jnp.power(ROPE_THETA, -(jnp.arange(0, half, 2, dtype=F32) / half))
    t = jnp.arange(smax)
    lane = jnp.arange(128)
    first = (lane % 64) < 32

    def tables(ang):
        c = jnp.cos(ang)
        s = jnp.sin(ang)
        return c, jnp.where(first[None, :], -s, 0.0), jnp.where(first[None, :], 0.0, s)

    inv128 = jnp.tile(inv, 4)[None, :]
    row = (t // GRID_W).astype(F32)[:, None]
    col = (t % GRID_W).astype(F32)[:, None]
    ang_a = jnp.where((lane < 64)[None, :], row * inv128, col * inv128)
    ang_b = t.astype(F32)[:, None] * inv128
    return tables(ang_a), tables(ang_b)


def _qk_prep(proj, tabs_a, tabs_b, gaq, gak, gbq, gbk, groups, tm=512):
    t = proj.shape[0]
    lo, _, _ = _seq_table(groups, tm)
    pos_blk = jnp.asarray((np.arange(t // tm) * tm - lo) // tm, jnp.int32)
    bd = jnp.asarray(np.kron(np.eye(4), np.full((64, 64), 1.0 / 64)), BF16)

    def col(width, start):
        return pl.BlockSpec((tm, width), lambda i, p: (i, start // width))

    tab = pl.BlockSpec((tm, 128), lambda i, p: (p[i], 0))

    def full(shape):
        return pl.BlockSpec(shape, lambda i, p: (0, 0))

    def out(width):
        return pl.BlockSpec((tm, width), lambda i, p: (i, 0))

    grid_spec = pltpu.PrefetchScalarGridSpec(
        num_scalar_prefetch=1,
        grid=(t // tm,),
        in_specs=[col(1024, COL_AQ), col(256, COL_AK), col(256, COL_AV), col(768, COL_BQ), col(768, COL_BK),
                  tab, tab, tab, tab, tab, tab,
                  full((1, 128)), full((1, 128)), full((1, 768)), full((1, 768)), full((256, 256))],
        out_specs=[pl.BlockSpec((A_KV_HEADS, A_GROUP, tm, 256), lambda i, p: (0, 0, i, 0)),
                   out(512), out(512), out(768), out(768)],
    )
    return pl.pallas_call(
        _prep_kernel,
        grid_spec=grid_spec,
        out_shape=[jax.ShapeDtypeStruct((A_KV_HEADS, A_GROUP, t, 256), BF16)]
        + [jax.ShapeDtypeStruct((t, w), BF16) for w in (512, 512, 768, 768)],
        compiler_params=_cparams(1),
    )(pos_blk, proj, proj, proj, proj, proj, *tabs_a, *tabs_b, gaq, gak, gbq, gbk, bd)


def _flash_kernel(q_ref, k_ref, v_ref, o_ref, acc_sc, m_sc, *, tq, tk, nk):
    rows = A_GROUP * tq

    def chunk(c):
        return pl.ds(pl.multiple_of(c * tk, tk), tk)

    def scores(c):
        return _dot_nt(q_ref[...].reshape(rows, 256), k_ref[chunk(c), :])

    def accumulate(c, carry):
        p = jnp.exp2(scores(c)).astype(BF16)
        acc_sc[...] += _dot(p, v_ref[chunk(c), :])
        return carry

    acc_sc[...] = jnp.zeros(acc_sc.shape, F32)
    lax.fori_loop(0, nk, accumulate, 0, unroll=2 if nk % 2 == 0 else 1)

    @pl.when(jnp.min(acc_sc[:, 128:129]) < ROW_SUM_FLOOR)
    def _():
        def row_max(c, carry):
            m_sc[...] = jnp.maximum(m_sc[...], jnp.max(scores(c), axis=-1, keepdims=True))
            return carry

        def accumulate_exact(c, carry):
            p = jnp.exp2(scores(c) - m_sc[:, :1]).astype(BF16)
            acc_sc[...] += _dot(p, v_ref[chunk(c), :])
            return carry

        m_sc[...] = jnp.full(m_sc.shape, -jnp.inf, F32)
        lax.fori_loop(0, nk, row_max, 0)
        acc_sc[...] = jnp.zeros(acc_sc.shape, F32)
        lax.fori_loop(0, nk, accumulate_exact, 0)

    for h in range(A_GROUP):
        acc = acc_sc[h * tq:(h + 1) * tq, :]
        o_ref[:, h * 128:(h + 1) * 128] = (acc[:, :128] / acc[:, 128:129]).astype(o_ref.dtype)


def _flash_group(aq, ak, av, o_prev, tok_off, b, s, tq, tk):
    t = ak.shape[0]
    assert tok_off % s == 0 and s % tq == 0 and s % tk == 0
    nq = s // tq
    qoff, soff = tok_off // tq, tok_off // s
    in_specs = [
        pl.BlockSpec((None, A_GROUP, tq, 256), lambda bb, g, i: (g, 0, qoff + bb * nq + i, 0)),
        pl.BlockSpec((s, 256), lambda bb, g, i: (soff + bb, g), pipeline_mode=pl.Buffered(1)),
        pl.BlockSpec((s, 256), lambda bb, g, i: (soff + bb, g), pipeline_mode=pl.Buffered(1)),
    ]
    args = [aq, ak, av]
    aliases = {}
    kernel = functools.partial(_flash_kernel, tq=tq, tk=tk, nk=s // tk)
    if o_prev is not None:
        in_specs.append(pl.BlockSpec(memory_space=pl.ANY))
        args.append(o_prev)
        aliases = {3: 0}
        inner = kernel

        def kernel(q_ref, k_ref, v_ref, prev_ref, o_ref, acc_sc, m_sc):
            del prev_ref
            inner(q_ref, k_ref, v_ref, o_ref, acc_sc, m_sc)

    return pl.pallas_call(
        kernel,
        grid=(b, A_KV_HEADS, nq),
        in_specs=in_specs,
        out_specs=pl.BlockSpec((tq, 512), lambda bb, g, i: (qoff + bb * nq + i, g)),
        out_shape=jax.ShapeDtypeStruct((t, 1024), BF16),
        scratch_shapes=[pltpu.VMEM((A_GROUP * tq, 256), F32), pltpu.VMEM((A_GROUP * tq, 128), F32)],
        input_output_aliases=aliases,
        compiler_params=_cparams(3),
    )(*args)


def _attention_a(aq, ak, av, groups, tq=512, tk=512):
    out = None
    off = 0
    for (b, s) in groups:
        out = _flash_group(aq, ak, av, out, off, b, s, min(tq, s), min(tk, s))
        off += b * s
    return out


def _band_tile(q, k, v, valid):
    tq = q.shape[0]
    nk = tq + 2 * B_SIDE
    rel = (lax.broadcasted_iota(jnp.int32, (tq, nk), 1) - B_SIDE) - lax.broadcasted_iota(jnp.int32, (tq, nk), 0)
    mask = (jnp.abs(rel) <= B_SIDE) & valid
    head_of_lane = lax.broadcasted_iota(jnp.int32, (1, B_GROUP_WIDTH), 1) // B_HEAD_DIM
    out = jnp.zeros((tq, B_GROUP_WIDTH), F32)
    lse_out = jnp.zeros((tq, B_GROUP_WIDTH), F32)
    hms = [head_of_lane == h for h in range(B_HEADS)]
    ss = [_dot_nt(jnp.where(hm, q, jnp.zeros_like(q)), k) for hm in hms]
    ps, lses = [], []
    for s in ss:
        s = jnp.where(mask, s, NEG_BIG)
        m = jnp.max(s, axis=-1, keepdims=True)
        e = jnp.where(mask, jnp.exp(s - m), 0.0)
        den = jnp.sum(e, axis=-1, keepdims=True)
        ps.append((e / den).astype(BF16))
        lses.append(m + jnp.log(den))
    os_ = [_dot(p, v) for p in ps]
    for hm, o, lse in zip(hms, os_, lses):
        out = jnp.where(hm, o, out)
        lse_out = jnp.where(hm, lse, lse_out)
    return out, lse_out


def _band_kernel(lo_ref, hi_ref, q_ref, kp_ref, kc_ref, kn_ref, vp_ref, vc_ref, vn_ref, o_ref, lse_ref,
                 *scratch, tq, dil):
    i = pl.program_id(0)
    span = tq * dil
    halo = B_SIDE * dil
    nk = tq + 2 * B_SIDE
    t0 = i * span
    lo = lo_ref[i]
    hi = hi_ref[i]
    col = lax.broadcasted_iota(jnp.int32, (tq, nk), 1)

    if dil == 1:
        k = jnp.concatenate([kp_ref[...], kc_ref[...], kn_ref[...]], axis=0)
        v = jnp.concatenate([vp_ref[...], vc_ref[...], vn_ref[...]], axis=0)
        kpos = t0 - halo + col
        out, lse = _band_tile(q_ref[...], k, v, (kpos >= lo) & (kpos < hi))
        o_ref[...] = out.astype(o_ref.dtype)
        lse_ref[...] = lse
        return

    qn_sc, kn_sc, vn_sc, on_sc, ln_sc = scratch
    def stage(dst, row0, src_ref):
        x = src_ref[...].astype(F32)
        n = x.shape[0]
        dst[0, row0:row0 + n, :] = x[:, :128]
        dst[1, row0:row0 + n, :] = x[:, 128:]

    def pick(src, r, n):
        rows = pl.ds(r, n, stride=dil)
        return jnp.concatenate([src[0, rows, :], src[1, rows, :]], axis=1).astype(BF16)

    def put(dst, r, x):
        rows = pl.ds(r, tq, stride=dil)
        dst[0, rows, :] = x[:, :128]
        dst[1, rows, :] = x[:, 128:]

    stage(qn_sc, 0, q_ref)
    for dst, (p_ref, c_ref, n_ref) in ((kn_sc, (kp_ref, kc_ref, kn_ref)), (vn_sc, (vp_ref, vc_ref, vn_ref))):
        stage(dst, 0, p_ref)
        stage(dst, halo, c_ref)
        stage(dst, halo + span, n_ref)

    def residue(r, carry):
        kpos = t0 - halo + r + dil * col
        out, lse = _band_tile(pick(qn_sc, r, tq), pick(kn_sc, r, nk), pick(vn_sc, r, nk),
                              (kpos >= lo) & (kpos < hi))
        put(on_sc, r, out)
        put(ln_sc, r, lse)
        return carry

    lax.fori_loop(0, dil, residue, 0)
    for half in range(2):
        lanes = slice(half * 128, (half + 1) * 128)
        o_ref[:, lanes] = on_sc[half].astype(o_ref.dtype)
        lse_ref[:, lanes] = ln_sc[half]


def _band_attention(bq, bk, proj, grp, dil, groups, tq=128):
    t = bq.shape[0]
    span = tq * dil
    halo = B_SIDE * dil
    lo, hi, _ = _seq_table(groups, span)
    r_halo = span // halo
    n_halo = t // halo
    vblk = COL_BV // 256 + grp

    def blk(rows, row_fn, colblk):
        return pl.BlockSpec((rows, 256), lambda i, lo_, hi_: (row_fn(i), colblk))

    prev_fn = lambda i: jnp.maximum(i * r_halo - 1, 0)
    next_fn = lambda i: jnp.minimum((i + 1) * r_halo, n_halo - 1)
    cur_fn = lambda i: i
    scratch = []
    if dil > 1:
        scratch = [pltpu.VMEM((2, span, 128), F32), pltpu.VMEM((2, span + 2 * halo, 128), F32),
                   pltpu.VMEM((2, span + 2 * halo, 128), F32), pltpu.VMEM((2, span, 128), F32),
                   pltpu.VMEM((2, span, 128), F32)]
    grid_spec = pltpu.PrefetchScalarGridSpec(
        num_scalar_prefetch=2,
        grid=(t // span,),
        in_specs=[blk(span, cur_fn, grp), blk(halo, prev_fn, grp), blk(span, cur_fn, grp), blk(halo, next_fn, grp),
                  blk(halo, prev_fn, vblk), blk(span, cur_fn, vblk), blk(halo, next_fn, vblk)],
        out_specs=[blk(span, cur_fn, 0), blk(span, cur_fn, 0)],
        scratch_shapes=scratch,
    )
    return pl.pallas_call(
        functools.partial(_band_kernel, tq=tq, dil=dil),
        grid_spec=grid_spec,
        out_shape=[jax.ShapeDtypeStruct((t, 256), BF16), jax.ShapeDtypeStruct((t, 256), F32)],
        compiler_params=_cparams(1),
    )(jnp.asarray(lo), jnp.asarray(hi), bq, bk, bk, bk, proj, proj, proj)


def _segment_products(f, f_sc, a_sc):
    f_sc[...] = f
    f8 = [f_sc[pl.ds(j, 8, stride=8), :] for j in range(8)]
    one = jnp.ones_like(f8[0])
    pair = [f8[j] * f8[j + 1] for j in (0, 2, 4, 6)]
    lo4, hi4 = pair[0] * pair[1], pair[2] * pair[3]
    a2 = [one, f8[1], one, f8[3], one, f8[5], one, f8[7]]
    a4 = [f8[1], one, f8[2], pair[1], f8[5], one, f8[6], pair[3]]
    a8 = [f8[1] * pair[1], pair[1], f8[3], one, f8[4], pair[2], pair[2] * f8[6], hi4]
    p8 = [f8[0], pair[0], pair[0] * f8[2], lo4, lo4 * f8[4], lo4 * pair[2], lo4 * (pair[2] * f8[6]), lo4 * hi4]
    x8 = [a8[0] * hi4, pair[1] * hi4, f8[3] * hi4, hi4, f8[5] * pair[3], pair[3], f8[7], one]
    for slot, arr in enumerate((a2, a4, a8, p8, x8)):
        for j in range(8):
            a_sc[slot, pl.ds(j, 8, stride=8), :] = arr[j]
    out = [a_sc[0], a_sc[1], a_sc[2]]
    p, x = a_sc[3], a_sc[4]
    for n in (8, 16, 32):
        n2 = 2 * n
        ap, pp, xp = [], [], []
        for s0 in range(0, C_CHUNK, n2):
            tot_l = p[s0 + n - 1:s0 + n]
            tot_r = p[s0 + n2 - 1:s0 + n2]
            ap += [x[s0:s0 + n], p[s0 + n:s0 + n2]]
            pp += [p[s0:s0 + n], p[s0 + n:s0 + n2] * tot_l]
            xp += [x[s0:s0 + n] * tot_r, x[s0 + n:s0 + n2]]
        out.append(jnp.concatenate(ap, axis=0))
        p = jnp.concatenate(pp, axis=0)
        x = jnp.concatenate(xp, axis=0)
    return out, p, x


def _level_masks():
    tt = lax.broadcasted_iota(jnp.int32, (C_CHUNK, C_CHUNK), 0)
    ss = lax.broadcasted_iota(jnp.int32, (C_CHUNK, C_CHUNK), 1)
    masks = []
    for shift in range(1, 7):
        n2 = 1 << shift
        n = n2 // 2
        masks.append(((tt >> shift) == (ss >> shift)) & ((tt & (n2 - 1)) >= n) & ((ss & (n2 - 1)) < n))
    return masks


def _hgrn_gates(z, cq, lb, f_sc, a_sc):
    zc = jnp.clip(z.astype(F32), -GATE_CLIP, GATE_CLIP)
    e = jnp.exp(-zc)
    sg = 1.0 / (1.0 + e)
    f = lb + (1.0 - lb) * sg
    k = (1.0 - lb) * (e * sg)
    cqf = cq.astype(F32)
    q = cqf * _sigmoid(cqf)
    levels, p64, x64 = _segment_products(f, f_sc, a_sc)
    return levels, p64, x64, q, k


def _hgrn_kernel(reset_ref, rev_ref, zf_ref, qf_ref, if_ref, zb_ref, qb_ref, ib_ref, lbf_ref, lbb_ref,
                 of_ref, ob_ref, st_sc, f_sc, a_sc, zr_sc, qr_sc, ir_sc, or_sc, *, n_chunks, heads):
    del rev_ref
    jblk = pl.program_id(1)

    @pl.when(reset_ref[jblk] == 1)
    def _():
        st_sc[...] = jnp.zeros(st_sc.shape, F32)

    rr = lax.broadcasted_iota(jnp.int32, (C_CHUNK, C_CHUNK), 0)
    cc = lax.broadcasted_iota(jnp.int32, (C_CHUNK, C_CHUNK), 1)
    flip = jnp.where(rr + cc == C_CHUNK - 1, 1.0, 0.0).astype(BF16)
    masks = _level_masks()

    def rows(c):
        return pl.ds(pl.multiple_of(c * C_CHUNK, C_CHUNK), C_CHUNK)

    def flip_in(c, carry):
        zr_sc[rows(c), :] = _dot(flip, zb_ref[rows(c), :]).astype(BF16)
        qr_sc[rows(c), :] = _dot(flip, qb_ref[rows(c), :]).astype(BF16)
        ir_sc[rows(c), :] = _dot(flip, ib_ref[rows(c), :]).astype(BF16)
        return carry

    lax.fori_loop(0, n_chunks, flip_in, 0, unroll=True)

    scans = []
    for h in range(heads):
        cols = slice(h * C_HEAD_DIM, (h + 1) * C_HEAD_DIM)
        scans.append((0, h, cols, zf_ref, qf_ref, if_ref, lbf_ref[:, cols], of_ref))
        scans.append((1, h, cols, zr_sc, qr_sc, ir_sc, lbb_ref[:, cols], or_sc))

    def body(c, carry):
        sl = (rows(c), rows(n_chunks - 1 - c))
        gates = [_hgrn_gates(z_ref[sl[d], cols], q_ref[sl[d], cols], lb, f_sc.at[d, h], a_sc.at[d, h])
                 for (d, h, cols, z_ref, q_ref, _, lb, _) in scans]
        ci = [i_ref[sl[d], cols] for (d, _, cols, _, _, i_ref, _, _) in scans]
        st = [st_sc[d, h] for (d, h, *_) in scans]
        scores = [jnp.zeros((C_CHUNK, C_CHUNK), F32) for _ in scans]
        for lvl, mask in enumerate(masks):
            for n, (levels, _, _, q, k) in enumerate(gates):
                a = levels[lvl]
                scores[n] = jnp.where(mask, _dot_nt((q * a).astype(BF16), (k * a).astype(BF16)), scores[n])
        inter = [_dot_nt((q * p64).astype(BF16), st[n].astype(BF16)) for n, (_, p64, _, q, _) in enumerate(gates)]
        update = [_dot_tn(ci[n], (k * x64).astype(BF16)) for n, (_, _, x64, _, k) in enumerate(gates)]
        intra = [_dot(scores[n].astype(BF16), ci[n]) for n in range(len(scans))]
        for n, (d, h, cols, _, _, _, _, o_ref) in enumerate(scans):
            _, p64, _, q, k = gates[n]
            diag = jnp.sum(q * k, axis=-1, keepdims=True)
            o_ref[sl[d], cols] = (intra[n] + diag * ci[n].astype(F32) + inter[n]).astype(o_ref.dtype)
            st_sc[d, h] = st[n] * p64[C_CHUNK - 1:C_CHUNK] + update[n]
        return carry

    lax.fori_loop(0, n_chunks, body, 0, unroll=2)

    def flip_out(c, carry):
        ob_ref[rows(c), :] = _dot(flip, or_sc[rows(c), :]).astype(ob_ref.dtype)
        return carry

    lax.fori_loop(0, n_chunks, flip_out, 0, unroll=True)


def _hgrn(proj, lbf, lbb, groups, blk=512, heads=4):
    t = proj.shape[0]
    blk = min(blk, min(s for _, s in groups))
    lo, hi, _ = _seq_table(groups, blk)
    idx = np.arange(t // blk)
    reset = jnp.asarray((idx * blk == lo).astype(np.int32))
    rev = jnp.asarray((lo // blk + hi // blk - 1 - idx).astype(np.int32))
    width = heads * C_HEAD_DIM

    def fwd(start):
        return pl.BlockSpec((blk, width), lambda h, j, rs, rv: (j, start // width + h))

    def bwd(start):
        return pl.BlockSpec((blk, width), lambda h, j, rs, rv: (rv[j], start // width + h))

    lbs = pl.BlockSpec((1, width), lambda h, j, rs, rv: (0, h))
    grid_spec = pltpu.PrefetchScalarGridSpec(
        num_scalar_prefetch=2,
        grid=(C_HEADS // heads, t // blk),
        in_specs=[fwd(COL_CFF), fwd(COL_CQ), fwd(COL_CI), bwd(COL_CFB), bwd(COL_CQ), bwd(COL_CI), lbs, lbs],
        out_specs=[fwd(0), bwd(0)],
        scratch_shapes=[pltpu.VMEM((2, heads, C_HEAD_DIM, C_HEAD_DIM), F32),
                        pltpu.VMEM((2, heads, C_CHUNK, C_HEAD_DIM), F32),
                        pltpu.VMEM((2, heads, 5, C_CHUNK, C_HEAD_DIM), F32)]
        + [pltpu.VMEM((blk, width), BF16) for _ in range(4)],
    )
    return pl.pallas_call(
        functools.partial(_hgrn_kernel, n_chunks=blk // C_CHUNK, heads=heads),
        grid_spec=grid_spec,
        out_shape=[jax.ShapeDtypeStruct((t, 1024), BF16), jax.ShapeDtypeStruct((t, 1024), BF16)],
        compiler_params=_cparams(2),
    )(reset, rev, proj, proj, proj, proj, proj, proj, lbf, lbb)


def _merge_kernel(x_ref, oa_ref, ob0_ref, ob1_ref, ob2_ref, l0_ref, l1_ref, l2_ref, ocf_ref, ocb_ref,
                  cg_ref, gates_ref, gn_ref, wa_ref, wb_ref, wc_ref, wm_ref, o_ref):
    l0, l1, l2 = l0_ref[...], l1_ref[...], l2_ref[...]
    m = jnp.maximum(jnp.maximum(l0, l1), l2)
    w0, w1, w2 = jnp.exp(l0 - m), jnp.exp(l1 - m), jnp.exp(l2 - m)
    ob = (w0 * ob0_ref[...].astype(F32) + w1 * ob1_ref[...].astype(F32)
          + w2 * ob2_ref[...].astype(F32)) / (w0 + w1 + w2)
    oc = ocf_ref[...].astype(F32) + ocb_ref[...].astype(F32)
    cg = cg_ref[...].astype(F32)
    ocn = _rms(oc, gn_ref[...]) * (cg * _sigmoid(cg))
    ya = _dot(oa_ref[...], wa_ref[...])
    yb = _dot(ob.astype(BF16), wb_ref[...])
    yc = _dot(ocn.astype(BF16), wc_ref[...])
    gates = _sigmoid(gates_ref[...].astype(F32))
    merged = gates[:, :1024] * ya + gates[:, 1024:2048] * yb + gates[:, 2048:] * yc
    o_ref[...] = x_ref[...] + _dot(merged.astype(BF16), wm_ref[...])


def _merge(x, oa, obs, lses, ocf, ocb, proj, gn, wa, wb, wc, wm, tm=512):
    t = x.shape[0]

    def rows(width):
        return pl.BlockSpec((tm, width), lambda i: (i, 0))

    def full(shape):
        return pl.BlockSpec(shape, lambda i: (0, 0))

    return pl.pallas_call(
        _merge_kernel,
        grid=(t // tm,),
        in_specs=[rows(1024), rows(1024), rows(256), rows(256), rows(256), rows(256), rows(256), rows(256),
                  rows(1024), rows(1024),
                  pl.BlockSpec((tm, 1024), lambda i: (i, COL_CG // 1024)),
                  pl.BlockSpec((tm, 3072), lambda i: (i, COL_GATES // 3072)),
                  full((1, 1024)), full((1024, 1024)), full((256, 1024)), full((1024, 1024)), full((1024, 1024))],
        out_specs=rows(1024),
        out_shape=jax.ShapeDtypeStruct((t, 1024), F32),
        compiler_params=_cparams(1),
    )(x, oa, *obs, *lses, ocf, ocb, proj, proj, gn, wa, wb, wc, wm)


def _mem_kv_kernel(mem_ref, g_ref, w_ref, gk_ref, k_ref, v_ref):
    u = _rms(mem_ref[0], g_ref[...]).astype(BF16)
    kv = _dot(u, w_ref[...])
    for h in range(X_HEADS):
        sl = slice(h * X_HEAD_DIM, (h + 1) * X_HEAD_DIM)
        k_ref[0, :, sl] = _rms(kv[:, sl], gk_ref[...]).astype(BF16)
    v_ref[0] = kv[:, D_MODEL:].astype(BF16)


def _mem_kv(mem, g, wkv, gk):
    nb, m, d = mem.shape
    return pl.pallas_call(
        _mem_kv_kernel,
        grid=(nb,),
        in_specs=[pl.BlockSpec((1, m, d), lambda b: (b, 0, 0)),
                  pl.BlockSpec((1, d), lambda b: (0, 0)),
                  pl.BlockSpec((d, 2 * d), lambda b: (0, 0)),
                  pl.BlockSpec((1, X_HEAD_DIM), lambda b: (0, 0))],
        out_specs=[pl.BlockSpec((1, m, d), lambda b: (b, 0, 0)), pl.BlockSpec((1, m, d), lambda b: (b, 0, 0))],
        out_shape=[jax.ShapeDtypeStruct((nb, m, d), BF16), jax.ShapeDtypeStruct((nb, m, d), BF16)],
        compiler_params=_cparams(1),
    )(mem, g, wkv, gk)


def _cross_kernel(sid_ref, x_ref, g_ref, wq_ref, gq_ref, k_ref, v_ref, wo_ref, o_ref):
    del sid_ref
    x = x_ref[...]
    u = _rms(x, g_ref[...]).astype(BF16)
    q = _dot(u, wq_ref[...])
    scale = X_HEAD_DIM ** -0.5
    outs = []
    for h in range(X_HEADS):
        sl = slice(h * X_HEAD_DIM, (h + 1) * X_HEAD_DIM)
        qh = (_rms(q[:, sl], gq_ref[...]) * scale).astype(BF16)
        s = _dot_nt(qh, k_ref[0, :, sl])
        m = jnp.max(s, axis=-1, keepdims=True)
        e = jnp.exp(s - m)
        p = (e / jnp.sum(e, axis=-1, keepdims=True)).astype(BF16)
        outs.append(_dot(p, v_ref[0, :, sl]).astype(BF16))
    o = jnp.concatenate(outs, axis=1)
    o_ref[...] = x + _dot(o, wo_ref[...])


def _cross(x, g, wq, gq, k, v, wo, groups, tm=512):
    t, d = x.shape
    _, _, sid = _seq_table(groups, tm)
    m = k.shape[1]

    def full(shape):
        return pl.BlockSpec(shape, lambda i, s: (0, 0))

    grid_spec = pltpu.PrefetchScalarGridSpec(
        num_scalar_prefetch=1,
        grid=(t // tm,),
        in_specs=[pl.BlockSpec((tm, d), lambda i, s: (i, 0)), full((1, d)), full((d, d)), full((1, X_HEAD_DIM)),
                  pl.BlockSpec((1, m, d), lambda i, s: (s[i], 0, 0)),
                  pl.BlockSpec((1, m, d), lambda i, s: (s[i], 0, 0)),
                  full((d, d))],
        out_specs=pl.BlockSpec((tm, d), lambda i, s: (i, 0)),
    )
    return pl.pallas_call(
        _cross_kernel,
        grid_spec=grid_spec,
        out_shape=jax.ShapeDtypeStruct((t, d), F32),
        compiler_params=_cparams(1),
    )(jnp.asarray(sid), x, g, wq, gq, k, v, wo)


def _ffn_kernel(first_ref, last_ref, x_ref, xp_ref, xn_ref, g_ref, wup_ref, cw_ref, cb_ref, wdn_ref, o_ref, u_sc,
                *, tm):
    i = pl.program_id(0)
    x = x_ref[...]
    g = g_ref[...]
    u_sc[0:tm, :] = _rms(x, g).astype(BF16)
    u_sc[tm:tm + 8, :] = _rms(xp_ref[...], g).astype(BF16)
    u_sc[tm + 8:tm + 16, :] = _rms(xn_ref[...], g).astype(BF16)
    has_prev = (first_ref[i] == 0).astype(F32)
    has_next = (last_ref[i] == 0).astype(F32)
    row = lax.broadcasted_iota(jnp.int32, (tm, FF_CHUNK), 0)
    first = row == 0
    last = row == tm - 1
    u = u_sc[...]

    def up(col0):
        return _dot(u, wup_ref[:, col0:col0 + FF_CHUNK])

    def conv(h, col0):
        hm = h[:tm]
        h_prev = jnp.where(first, h[tm + 7:tm + 8] * has_prev, pltpu.roll(hm, 1, 0))
        h_next = jnp.where(last, h[tm + 8:tm + 9] * has_next, pltpu.roll(hm, tm - 1, 0))
        w = cw_ref[:, col0:col0 + FF_CHUNK]
        return h_prev * w[0:1] + hm * w[1:2] + h_next * w[2:3] + cb_ref[:, col0:col0 + FF_CHUNK]

    n_chunks = D_FF // FF_CHUNK
    acc = x
    h_next = (up(0), up(D_FF))
    for c in range(n_chunks):
        h_a, h_g = h_next
        if c + 1 < n_chunks:
            h_next = (up((c + 1) * FF_CHUNK), up(D_FF + (c + 1) * FF_CHUNK))
        a = conv(h_a, c * FF_CHUNK)
        gg = conv(h_g, D_FF + c * FF_CHUNK)
        act = (a * (gg * _sigmoid(gg))).astype(BF16)
        acc = acc + _dot(act, wdn_ref[c * FF_CHUNK:(c + 1) * FF_CHUNK, :])
    o_ref[...] = acc


def _ffn(x, g, wup, cw, cb, wdn, groups, tm=512):
    t, d = x.shape
    lo, hi, _ = _seq_table(groups, tm)
    start = np.arange(t // tm) * tm
    is_first = jnp.asarray((start == lo).astype(np.int32))
    is_last = jnp.asarray((start + tm == hi).astype(np.int32))
    r8 = tm // 8
    n8 = t // 8

    def full(shape):
        return pl.BlockSpec(shape, lambda i, a, b: (0, 0))

    grid_spec = pltpu.PrefetchScalarGridSpec(
        num_scalar_prefetch=2,
        grid=(t // tm,),
        in_specs=[pl.BlockSpec((tm, d), lambda i, a, b: (i, 0)),
                  pl.BlockSpec((8, d), lambda i, a, b: (jnp.maximum(i * r8 - 1, 0), 0)),
                  pl.BlockSpec((8, d), lambda i, a, b: (jnp.minimum((i + 1) * r8, n8 - 1), 0)),
                  full((1, d)), full((d, 2 * D_FF)), full((3, 2 * D_FF)), full((1, 2 * D_FF)), full((D_FF, d))],
        out_specs=pl.BlockSpec((tm, d), lambda i, a, b: (i, 0)),
        scratch_shapes=[pltpu.VMEM((tm + 16, d), BF16)],
    )
    return pl.pallas_call(
        functools.partial(_ffn_kernel, tm=tm),
        grid_spec=grid_spec,
        out_shape=jax.ShapeDtypeStruct((t, d), F32),
        compiler_params=_cparams(1),
    )(is_first, is_last, x, x, x, g, wup, cw, cb, wdn)


def _permute_w_in(w):
    aq, ak, av = w[:, 0:1024], w[:, 1024:1280], w[:, 1280:1536]
    bq, bk, bv = w[:, 1536:2304], w[:, 2304:3072], w[:, 3072:3840]
    cq, ci, cff, cfb, cg = (w[:, 3840 + 1024 * n:4864 + 1024 * n] for n in range(5))
    gates = w[:, 8960:12032]
    pad = jnp.zeros((w.shape[0], 256), w.dtype)
    return jnp.concatenate([gates, bq, bk, bv, ak, av, pad, aq, cq, ci, cff, cfb, cg], axis=1).astype(BF16)


def _lower_bound(raw, layer):
    p = jax.nn.softmax(raw.astype(F32), axis=0)
    return (jnp.cumsum(p, axis=0) - p[0])[layer][None, :]


def _trunk(x, mem, groups, p):
    depth = p['g_mix'].shape[0]
    smax = max(s for _, s in groups)
    tabs_a, tabs_b = _rope_tables(smax)
    for layer in range(depth):
        row = lambda name: p[name][layer][None, :].astype(F32)
        wbf = lambda name: p[name][layer].astype(BF16)
        proj = _norm_proj(x, row('g_mix'), _permute_w_in(p['w_in'][layer]))
        gbq = jnp.broadcast_to(p['b_gq'][layer][:, None, :], (3, B_HEADS, B_HEAD_DIM)).reshape(1, 768)
        gbk = jnp.broadcast_to(p['b_gk'][layer][:, None, :], (3, B_HEADS, B_HEAD_DIM)).reshape(1, 768)
        aq, ak, av, bq, bk = _qk_prep(proj, tabs_a, tabs_b, row('a_gq'), row('a_gk'),
                                      gbq.astype(F32), gbk.astype(F32), groups)
        oa = _attention_a(aq, ak, av, groups)
        obs, lses = [], []
        for grp, (_, dil) in enumerate(B_PATTERNS):
            o, lse = _band_attention(bq, bk, proj, grp, dil, groups)
            obs.append(o)
            lses.append(lse)
        ocf, ocb = _hgrn(proj, _lower_bound(p['c_lb_fwd'], layer), _lower_bound(p['c_lb_bwd'], layer), groups)
        x = _merge(x, oa, obs, lses, ocf, ocb, proj, row('c_gnorm'),
                   wbf('w_br_a'), wbf('w_br_b'), wbf('w_br_c'), wbf('w_mix_out'))
        mk, mv = _mem_kv(mem, row('g_mem'), wbf('x_wkv'), row('x_gk'))
        x = _cross(x, row('g_cross'), wbf('x_wq'), row('x_gq'), mk, mv, wbf('x_wo'), groups)
        x = _ffn(x, row('g_ffn'), wbf('f_wup'), p['f_conv'][layer].astype(F32), row('f_conv_b'),
                 wbf('f_wdown'), groups)
    return x


def kernel(x_prompt, x_sample, mem_prompt, mem_sample, g_mix, w_in, a_gq, a_gk, b_gq, b_gk, c_lb_fwd, c_lb_bwd, c_gnorm, w_br_a, w_br_b, w_br_c, w_mix_out, g_cross, g_mem, x_wq, x_wkv, x_gq, x_gk, x_wo, g_ffn, f_wup, f_conv, f_conv_b, f_wdown):
    p = dict(g_mix=g_mix, w_in=w_in, a_gq=a_gq, a_gk=a_gk, b_gq=b_gq, b_gk=b_gk,
             c_lb_fwd=c_lb_fwd, c_lb_bwd=c_lb_bwd, c_gnorm=c_gnorm, w_br_a=w_br_a, w_br_b=w_br_b,
             w_br_c=w_br_c, w_mix_out=w_mix_out, g_cross=g_cross, g_mem=g_mem, x_wq=x_wq,
             x_wkv=x_wkv, x_gq=x_gq, x_gk=x_gk, x_wo=x_wo, g_ffn=g_ffn, f_wup=f_wup,
             f_conv=f_conv, f_conv_b=f_conv_b, f_wdown=f_wdown)
    groups = [(x_prompt.shape[0], x_prompt.shape[1]), (x_sample.shape[0], x_sample.shape[1])]
    d = x_prompt.shape[-1]
    x = jnp.concatenate([x_prompt.reshape(-1, d), x_sample.reshape(-1, d)], axis=0)
    mem = jnp.concatenate([mem_prompt, mem_sample], axis=0)
    y = _trunk(x, mem, groups, p)
    n_prompt = x_prompt.shape[0] * x_prompt.shape[1]
    return (y[:n_prompt].reshape(x_prompt.shape), y[n_prompt:].reshape(x_sample.shape))
```

```python
import functools

import numpy as np
import jax
import jax.numpy as jnp
from jax import lax
from jax.experimental import pallas as pl
from jax.experimental.pallas import tpu as pltpu

F32 = jnp.float32
BF16 = jnp.bfloat16

D_MODEL = 1024
GRID_W = 64
EPS = 1e-6
ROPE_THETA = 10000.0
NEG_BIG = -1e30
GATE_CLIP = 30.0
LOG2E = 1.4426950408889634
BOUND_SLACK = 1.0 + 2.0 ** -6
ROW_SUM_FLOOR = 2.0 ** -100

A_HEADS = 8
A_KV_HEADS = 2
A_HEAD_DIM = 128
A_GROUP = A_HEADS // A_KV_HEADS

B_PATTERNS = ((128, 1), (512, 4), (2048, 16))
B_HEADS = 4
B_HEAD_DIM = 64
B_GROUP_WIDTH = B_HEADS * B_HEAD_DIM
B_SIDE = 64
V_AUG_ROWS = A_HEAD_DIM + 16

C_HEADS = 8
C_HEAD_DIM = 128
C_CHUNK = 64

X_HEADS = 4
X_HEAD_DIM = D_MODEL // X_HEADS

D_FF = 2816
FF_CHUNK = 256

COL_GATES = 0
COL_BQ = 3072
COL_BK = 3840
COL_BV = 4608
COL_AK = 5376
COL_AV = 5632
COL_AQ = 6144
COL_CQ = 7168
COL_CI = 8192
COL_CFF = 9216
COL_CFB = 10240
COL_CG = 11264
PROJ_WIDTH = 12288

VMEM_LIMIT_BYTES = 56 * 1024 * 1024


def _cparams(n_axes, vmem=VMEM_LIMIT_BYTES):
    return pltpu.CompilerParams(dimension_semantics=("arbitrary",) * n_axes, vmem_limit_bytes=vmem)


def _dot(a, b):
    return jnp.dot(a, b, preferred_element_type=F32)


def _dot_nt(a, b):
    return lax.dot_general(a, b, (((1,), (1,)), ((), ())), preferred_element_type=F32)


def _dot_tn(a, b):
    return lax.dot_general(a, b, (((0,), (0,)), ((), ())), preferred_element_type=F32)


def _rms(x, g):
    ms = jnp.mean(x * x, axis=-1, keepdims=True)
    return x * lax.rsqrt(ms + EPS) * g


def _sigmoid(x):
    return 1.0 / (1.0 + jnp.exp(-x))


def _seq_table(groups, tile):
    lo, hi, sid = [], [], []
    off, s_idx = 0, 0
    for (b, s) in groups:
        assert s % tile == 0
        for _ in range(b):
            for _ in range(s // tile):
                lo.append(off)
                hi.append(off + s)
                sid.append(s_idx)
            off += s
            s_idx += 1
    return np.asarray(lo, np.int32), np.asarray(hi, np.int32), np.asarray(sid, np.int32)


def _proj_kernel(x_ref, g_ref, w_ref, o_ref, u_sc):
    @pl.when(pl.program_id(1) == 0)
    def _():
        u_sc[...] = _rms(x_ref[...], g_ref[...]).astype(BF16)

    o_ref[...] = _dot(u_sc[...], w_ref[...]).astype(o_ref.dtype)


def _norm_proj(x, g, w, tm=1024, tn=1024):
    t, k = x.shape
    n = w.shape[1]
    return pl.pallas_call(
        _proj_kernel,
        grid=(t // tm, n // tn),
        in_specs=[
            pl.BlockSpec((tm, k), lambda i, j: (i, 0)),
            pl.BlockSpec((1, k), lambda i, j: (0, 0)),
            pl.BlockSpec((k, tn), lambda i, j: (0, j)),
        ],
        out_specs=pl.BlockSpec((tm, tn), lambda i, j: (i, j)),
        out_shape=jax.ShapeDtypeStruct((t, n), BF16),
        scratch_shapes=[pltpu.VMEM((tm, k), BF16)],
        compiler_params=_cparams(2),
    )(x, g, w)


def _rope(x, c, s1, s2):
    return x * c + pltpu.roll(x, 96, 1) * s1 + pltpu.roll(x, 32, 1) * s2


def _prep_kernel(pos_ref, aq_ref, ak_ref, av_ref, bq_ref, bk_ref,
                 ca_ref, s1a_ref, s2a_ref, cb_ref, s1b_ref, s2b_ref,
                 gaq_ref, gak_ref, gbq_ref, gbk_ref, bd_ref,
                 oaq_ref, oak_ref, oav_ref, obq_ref, obk_ref):
    del pos_ref
    ca, s1a, s2a = ca_ref[...], s1a_ref[...], s2a_ref[...]
    cb, s1b, s2b = cb_ref[...], s1b_ref[...], s2b_ref[...]
    a_scale = A_HEAD_DIM ** -0.5 * LOG2E
    b_scale = B_HEAD_DIM ** -0.5
    tm = aq_ref.shape[0]
    lane0 = lax.broadcasted_iota(jnp.int32, (tm, 128), 1) == 0

    def a_head(x, g, scale):
        y = _rms(x.astype(F32), g)
        return _rope(y, ca, s1a, s2a) * scale

    k_bound = jnp.max(jnp.abs(gak_ref[...]), axis=-1, keepdims=True) * (A_HEAD_DIM ** 0.5 * BOUND_SLACK)
    for h in range(A_HEADS):
        sl = slice(h * 128, (h + 1) * 128)
        q = a_head(aq_ref[:, sl], gaq_ref[...], a_scale)
        bound = jnp.sqrt(jnp.sum(q * q, axis=-1, keepdims=True)) * k_bound
        qcols = slice((h % A_GROUP) * tm, (h % A_GROUP + 1) * tm)
        oaq_ref[h // A_GROUP, 0:128, qcols] = q.T.astype(BF16)
        oaq_ref[h // A_GROUP, 128:256, qcols] = jnp.where(lane0, -bound, 0.0).T.astype(BF16)
    for h in range(A_KV_HEADS):
        sl = slice(h * 128, (h + 1) * 128)
        oak_ref[:, h * 256:h * 256 + 128] = a_head(ak_ref[:, sl], gak_ref[...], 1.0).astype(BF16)
        oak_ref[:, h * 256 + 128:(h + 1) * 256] = jnp.where(lane0, 1.0, 0.0).astype(BF16)
        oav_ref[h, 0, 0:128, :] = av_ref[:, sl].astype(F32).T.astype(BF16)
        ones_row = lax.broadcasted_iota(jnp.int32, (V_AUG_ROWS - 128, tm), 0) == 0
        oav_ref[h, 0, 128:V_AUG_ROWS, :] = jnp.where(ones_row, 1.0, 0.0).astype(BF16)

    bd = bd_ref[...]

    def b_group(x_ref, g_ref, o_ref, scale):
        for grp in range(3):
            sl = slice(grp * 256, (grp + 1) * 256)
            x = x_ref[:, sl].astype(F32)
            ms = _dot((x * x).astype(BF16), bd)
            y = x * lax.rsqrt(ms + EPS) * g_ref[:, sl]
            for half in range(2):
                hs = slice(half * 128, (half + 1) * 128)
                o_ref[:, grp * 256 + half * 128:grp * 256 + (half + 1) * 128] = (
                    _rope(y[:, hs], cb, s1b, s2b) * scale).astype(BF16)

    b_group(bq_ref, gbq_ref, obq_ref, b_scale)
    b_group(bk_ref, gbk_ref, obk_ref, 1.0)


def _rope_tables(smax):
    half = A_HEAD_DIM // 2
    inv = jnp.power(ROPE_THETA, -(jnp.arange(0, half, 2, dtype=F32) / half))
    t = jnp.arange(smax)
    lane = jnp.arange(128)
    first = (lane % 64) < 32

    def tables(ang):
        c = jnp.cos(ang)
        s = jnp.sin(ang)
        return c, jnp.where(first[None, :], -s, 0.0), jnp.where(first[None, :], 0.0, s)

    inv128 = jnp.tile(inv, 4)[None, :]
    row = (t // GRID_W).astype(F32)[:, None]
    col = (t % GRID_W).astype(F32)[:, None]
    ang_a = jnp.where((lane < 64)[None, :], row * inv128, col * inv128)
    ang_b = t.astype(F32)[:, None] * inv128
    return tables(ang_a), tables(ang_b)


def _qk_prep(proj, tabs_a, tabs_b, gaq, gak, gbq, gbk, groups, tm=512):
    t = proj.shape[0]
    lo, _, _ = _seq_table(groups, tm)
    pos_blk = jnp.asarray((np.arange(t // tm) * tm - lo) // tm, jnp.int32)
    bd = jnp.asarray(np.kron(np.eye(4), np.full((64, 64), 1.0 / 64)), BF16)

    def col(width, start):
        return pl.BlockSpec((tm, width), lambda i, p: (i, start // width))

    tab = pl.BlockSpec((tm, 128), lambda i, p: (p[i], 0))

    def full(shape):
        return pl.BlockSpec(shape, lambda i, p: (0, 0))

    def out(width):
        return pl.BlockSpec((tm, width), lambda i, p: (i, 0))

    grid_spec = pltpu.PrefetchScalarGridSpec(
        num_scalar_prefetch=1,
        grid=(t // tm,),
        in_specs=[col(1024, COL_AQ), col(256, COL_AK), col(256, COL_AV), col(768, COL_BQ), col(768, COL_BK),
                  tab, tab, tab, tab, tab, tab,
                  full((1, 128)), full((1, 128)), full((1, 768)), full((1, 768)), full((256, 256))],
        out_specs=[pl.BlockSpec((A_KV_HEADS, 256, A_GROUP * tm), lambda i, p: (0, 0, i)),
                   out(512),
                   pl.BlockSpec((A_KV_HEADS, 1, V_AUG_ROWS, tm), lambda i, p: (0, i, 0, 0)),
                   out(768), out(768)],
    )
    return pl.pallas_call(
        _prep_kernel,
        grid_spec=grid_spec,
        out_shape=[jax.ShapeDtypeStruct((A_KV_HEADS, 256, A_GROUP * t), BF16),
                   jax.ShapeDtypeStruct((t, 512), BF16),
                   jax.ShapeDtypeStruct((A_KV_HEADS, t // tm, V_AUG_ROWS, tm), BF16),
                   jax.ShapeDtypeStruct((t, 768), BF16), jax.ShapeDtypeStruct((t, 768), BF16)],
        compiler_params=_cparams(1),
    )(pos_blk, proj, proj, proj, proj, proj, *tabs_a, *tabs_b, gaq, gak, gbq, gbk, bd)


def _flash_kernel(qt_ref, k_ref, vt_ref, o_ref, acc_sc, m_sc, *, tq, tk, nk):
    def scores_t(c):
        return _dot(k_ref[pl.ds(pl.multiple_of(c * tk, tk), tk), :], qt_ref[...])

    def accumulate(c, carry):
        p = jnp.exp2(scores_t(c)).astype(BF16)
        acc_sc[...] += _dot(vt_ref[c], p)
        return carry

    acc_sc[...] = jnp.zeros(acc_sc.shape, F32)
    lax.fori_loop(0, nk, accumulate, 0, unroll=2 if nk % 2 == 0 else 1)

    @pl.when(jnp.min(acc_sc[128:129, :]) < ROW_SUM_FLOOR)
    def _():
        def col_max(c, carry):
            m_sc[...] = jnp.maximum(m_sc[...], jnp.max(scores_t(c), axis=0, keepdims=True))
            return carry

        def accumulate_exact(c, carry):
            p = jnp.exp2(scores_t(c) - m_sc[0:1, :]).astype(BF16)
            acc_sc[...] += _dot(vt_ref[c], p)
            return carry

        m_sc[...] = jnp.full(m_sc.shape, -jnp.inf, F32)
        lax.fori_loop(0, nk, col_max, 0)
        acc_sc[...] = jnp.zeros(acc_sc.shape, F32)
        lax.fori_loop(0, nk, accumulate_exact, 0)

    out_t = acc_sc[0:128, :] / acc_sc[128:129, :]
    for h in range(A_GROUP):
        o_ref[:, h * 128:(h + 1) * 128] = out_t[:, h * tq:(h + 1) * tq].T.astype(o_ref.dtype)


def _flash_group(aq, ak, av, tok_off, b, s, tq, tk):
    assert tok_off % s == 0 and s % tq == 0 and s % tk == 0 and av.shape[-1] == tk
    nq = s // tq
    qoff, soff = tok_off // tq, tok_off // s
    in_specs = [
        pl.BlockSpec((None, 256, A_GROUP * tq), lambda bb, g, i: (g, 0, qoff + bb * nq + i)),
        pl.BlockSpec((s, 256), lambda bb, g, i: (soff + bb, g), pipeline_mode=pl.Buffered(1)),
        pl.BlockSpec((None, s // tk, V_AUG_ROWS, tk), lambda bb, g, i: (g, soff + bb, 0, 0),
                     pipeline_mode=pl.Buffered(1)),
    ]
    return pl.pallas_call(
        functools.partial(_flash_kernel, tq=tq, tk=tk, nk=s // tk),
        grid=(b, A_KV_HEADS, nq),
        in_specs=in_specs,
        out_specs=pl.BlockSpec((tq, 512), lambda bb, g, i: (bb * nq + i, g)),
        out_shape=jax.ShapeDtypeStruct((b * s, 1024), BF16),
        scratch_shapes=[pltpu.VMEM((V_AUG_ROWS, A_GROUP * tq), F32), pltpu.VMEM((8, A_GROUP * tq), F32)],
        compiler_params=_cparams(3),
    )(aq, ak, av)


def _attention_a(aq, ak, av, groups, tq=512, tk=512):
    outs = []
    off = 0
    for (b, s) in groups:
        outs.append(_flash_group(aq, ak, av, off, b, s, tq, tk))
        off += b * s
    return outs


def _band_tile(q, k, v, valid):
    tq = q.shape[0]
    nk = tq + 2 * B_SIDE
    rel = (lax.broadcasted_iota(jnp.int32, (tq, nk), 1) - B_SIDE) - lax.broadcasted_iota(jnp.int32, (tq, nk), 0)
    mask = (jnp.abs(rel) <= B_SIDE) & valid
    head_of_lane = lax.broadcasted_iota(jnp.int32, (1, B_GROUP_WIDTH), 1) // B_HEAD_DIM
    out = jnp.zeros((tq, B_GROUP_WIDTH), F32)
    lse_out = jnp.zeros((tq, B_GROUP_WIDTH), F32)
    hms = [head_of_lane == h for h in range(B_HEADS)]
    ss = [_dot_nt(jnp.where(hm, q, jnp.zeros_like(q)), k) for hm in hms]
    ps, lses = [], []
    for s in ss:
        s = jnp.where(mask, s, NEG_BIG)
        m = jnp.max(s, axis=-1, keepdims=True)
        e = jnp.where(mask, jnp.exp(s - m), 0.0)
        den = jnp.sum(e, axis=-1, keepdims=True)
        ps.append((e / den).astype(BF16))
        lses.append(m + jnp.log(den))
    os_ = [_dot(p, v) for p in ps]
    for hm, o, lse in zip(hms, os_, lses):
        out = jnp.where(hm, o, out)
        lse_out = jnp.where(hm, lse, lse_out)
    return out, lse_out


def _band_kernel(lo_ref, hi_ref, q_ref, kp_ref, kc_ref, kn_ref, vp_ref, vc_ref, vn_ref, o_ref, lse_ref,
                 *scratch, tq, dil):
    i = pl.program_id(0)
    span = tq * dil
    halo = B_SIDE * dil
    nk = tq + 2 * B_SIDE
    t0 = i * span
    lo = lo_ref[i]
    hi = hi_ref[i]
    col = lax.broadcasted_iota(jnp.int32, (tq, nk), 1)

    if dil == 1:
        k = jnp.concatenate([kp_ref[...], kc_ref[...], kn_ref[...]], axis=0)
        v = jnp.concatenate([vp_ref[...], vc_ref[...], vn_ref[...]], axis=0)
        kpos = t0 - halo + col
        out, lse = _band_tile(q_ref[...], k, v, (kpos >= lo) & (kpos < hi))
        o_ref[...] = out.astype(o_ref.dtype)
        lse_ref[...] = lse
        return

    qn_sc, kn_sc, vn_sc, on_sc, ln_sc = scratch
    def stage(dst, row0, src_ref):
        x = src_ref[...].astype(F32)
        n = x.shape[0]
        dst[0, row0:row0 + n, :] = x[:, :128]
        dst[1, row0:row0 + n, :] = x[:, 128:]

    def pick(src, r, n):
        rows = pl.ds(r, n, stride=dil)
        return jnp.concatenate([src[0, rows, :], src[1, rows, :]], axis=1).astype(BF16)

    def put(dst, r, x):
        rows = pl.ds(r, tq, stride=dil)
        dst[0, rows, :] = x[:, :128]
        dst[1, rows, :] = x[:, 128:]

    stage(qn_sc, 0, q_ref)
    for dst, (p_ref, c_ref, n_ref) in ((kn_sc, (kp_ref, kc_ref, kn_ref)), (vn_sc, (vp_ref, vc_ref, vn_ref))):
        stage(dst, 0, p_ref)
        stage(dst, halo, c_ref)
        stage(dst, halo + span, n_ref)

    def residue(r, carry):
        kpos = t0 - halo + r + dil * col
        out, lse = _band_tile(pick(qn_sc, r, tq), pick(kn_sc, r, nk), pick(vn_sc, r, nk),
                              (kpos >= lo) & (kpos < hi))
        put(on_sc, r, out)
        put(ln_sc, r, lse)
        return carry

    lax.fori_loop(0, dil, residue, 0)
    for half in range(2):
        lanes = slice(half * 128, (half + 1) * 128)
        o_ref[:, lanes] = on_sc[half].astype(o_ref.dtype)
        lse_ref[:, lanes] = ln_sc[half]


def _band_attention(bq, bk, proj, grp, dil, groups, tq=128):
    t = bq.shape[0]
    span = tq * dil
    halo = B_SIDE * dil
    lo, hi, _ = _seq_table(groups, span)
    r_halo = span // halo
    n_halo = t // halo
    vblk = COL_BV // 256 + grp

    def blk(rows, row_fn, colblk):
        return pl.BlockSpec((rows, 256), lambda i, lo_, hi_: (row_fn(i), colblk))

    prev_fn = lambda i: jnp.maximum(i * r_halo - 1, 0)
    next_fn = lambda i: jnp.minimum((i + 1) * r_halo, n_halo - 1)
    cur_fn = lambda i: i
    scratch = []
    if dil > 1:
        scratch = [pltpu.VMEM((2, span, 128), F32), pltpu.VMEM((2, span + 2 * halo, 128), F32),
                   pltpu.VMEM((2, span + 2 * halo, 128), F32), pltpu.VMEM((2, span, 128), F32),
                   pltpu.VMEM((2, span, 128), F32)]
    grid_spec = pltpu.PrefetchScalarGridSpec(
        num_scalar_prefetch=2,
        grid=(t // span,),
        in_specs=[blk(span, cur_fn, grp), blk(halo, prev_fn, grp), blk(span, cur_fn, grp), blk(halo, next_fn, grp),
                  blk(halo, prev_fn, vblk), blk(span, cur_fn, vblk), blk(halo, next_fn, vblk)],
        out_specs=[blk(span, cur_fn, 0), blk(span, cur_fn, 0)],
        scratch_shapes=scratch,
    )
    return pl.pallas_call(
        functools.partial(_band_kernel, tq=tq, dil=dil),
        grid_spec=grid_spec,
        out_shape=[jax.ShapeDtypeStruct((t, 256), BF16), jax.ShapeDtypeStruct((t, 256), F32)],
        compiler_params=_cparams(1),
    )(jnp.asarray(lo), jnp.asarray(hi), bq, bk, bk, bk, proj, proj, proj)


def _segment_products(f, f_sc, a_sc):
    f_sc[...] = f
    f8 = [f_sc[pl.ds(j, 8, stride=8), :] for j in range(8)]
    one = jnp.ones_like(f8[0])
    pair = [f8[j] * f8[j + 1] for j in (0, 2, 4, 6)]
    lo4, hi4 = pair[0] * pair[1], pair[2] * pair[3]
    a2 = [one, f8[1], one, f8[3], one, f8[5], one, f8[7]]
    a4 = [f8[1], one, f8[2], pair[1], f8[5], one, f8[6], pair[3]]
    a8 = [f8[1] * pair[1], pair[1], f8[3], one, f8[4], pair[2], pair[2] * f8[6], hi4]
    p8 = [f8[0], pair[0], pair[0] * f8[2], lo4, lo4 * f8[4], lo4 * pair[2], lo4 * (pair[2] * f8[6]), lo4 * hi4]
    x8 = [a8[0] * hi4, pair[1] * hi4, f8[3] * hi4, hi4, f8[5] * pair[3], pair[3], f8[7], one]
    for slot, arr in enumerate((a2, a4, a8, p8, x8)):
        for j in range(8):
            a_sc[slot, pl.ds(j, 8, stride=8), :] = arr[j]
    out = [a_sc[0], a_sc[1], a_sc[2]]
    p, x = a_sc[3], a_sc[4]
    for n in (8, 16, 32):
        n2 = 2 * n
        ap, pp, xp = [], [], []
        for s0 in range(0, C_CHUNK, n2):
            tot_l = p[s0 + n - 1:s0 + n]
            tot_r = p[s0 + n2 - 1:s0 + n2]
            ap += [x[s0:s0 + n], p[s0 + n:s0 + n2]]
            pp += [p[s0:s0 + n], p[s0 + n:s0 + n2] * tot_l]
            xp += [x[s0:s0 + n] * tot_r, x[s0 + n:s0 + n2]]
        out.append(jnp.concatenate(ap, axis=0))
        p = jnp.concatenate(pp, axis=0)
        x = jnp.concatenate(xp, axis=0)
    return out, p, x


def _level_masks():
    tt = lax.broadcasted_iota(jnp.int32, (C_CHUNK, C_CHUNK), 0)
    ss = lax.broadcasted_iota(jnp.int32, (C_CHUNK, C_CHUNK), 1)
    masks = []
    for shift in range(1, 7):
        n2 = 1 << shift
        n = n2 // 2
        masks.append(((tt >> shift) == (ss >> shift)) & ((tt & (n2 - 1)) >= n) & ((ss & (n2 - 1)) < n))
    return masks


def _hgrn_gates(z, cq, lb, f_sc, a_sc):
    zc = jnp.clip(z.astype(F32), -GATE_CLIP, GATE_CLIP)
    e = jnp.exp(-zc)
    sg = 1.0 / (1.0 + e)
    f = lb + (1.0 - lb) * sg
    k = (1.0 - lb) * (e * sg)
    cqf = cq.astype(F32)
    q = cqf * _sigmoid(cqf)
    levels, p64, x64 = _segment_products(f, f_sc, a_sc)
    return levels, p64, x64, q, k


def _hgrn_kernel(reset_ref, rev_ref, zf_ref, qf_ref, if_ref, zb_ref, qb_ref, ib_ref, lbf_ref, lbb_ref,
                 of_ref, ob_ref, st_sc, f_sc, a_sc, zr_sc, qr_sc, ir_sc, or_sc, *, n_chunks, heads):
    del rev_ref
    jblk = pl.program_id(1)

    @pl.when(reset_ref[jblk] == 1)
    def _():
        st_sc[...] = jnp.zeros(st_sc.shape, F32)

    rr = lax.broadcasted_iota(jnp.int32, (C_CHUNK, C_CHUNK), 0)
    cc = lax.broadcasted_iota(jnp.int32, (C_CHUNK, C_CHUNK), 1)
    flip = jnp.where(rr + cc == C_CHUNK - 1, 1.0, 0.0).astype(BF16)
    masks = _level_masks()

    def rows(c):
        return pl.ds(pl.multiple_of(c * C_CHUNK, C_CHUNK), C_CHUNK)

    def flip_in(c, carry):
        zr_sc[rows(c), :] = _dot(flip, zb_ref[rows(c), :]).astype(BF16)
        qr_sc[rows(c), :] = _dot(flip, qb_ref[rows(c), :]).astype(BF16)
        ir_sc[rows(c), :] = _dot(flip, ib_ref[rows(c), :]).astype(BF16)
        return carry

    lax.fori_loop(0, n_chunks, flip_in, 0, unroll=True)

    scans = []
    for h in range(heads):
        cols = slice(h * C_HEAD_DIM, (h + 1) * C_HEAD_DIM)
        scans.append((0, h, cols, zf_ref, qf_ref, if_ref, lbf_ref[:, cols], of_ref))
        scans.append((1, h, cols, zr_sc, qr_sc, ir_sc, lbb_ref[:, cols], or_sc))

    def body(c, carry):
        sl = (rows(c), rows(n_chunks - 1 - c))
        gates = [_hgrn_gates(z_ref[sl[d], cols], q_ref[sl[d], cols], lb, f_sc.at[d, h], a_sc.at[d, h])
                 for (d, h, cols, z_ref, q_ref, _, lb, _) in scans]
        ci = [i_ref[sl[d], cols] for (d, _, cols, _, _, i_ref, _, _) in scans]
        st = [st_sc[d, h] for (d, h, *_) in scans]
        scores = [jnp.zeros((C_CHUNK, C_CHUNK), F32) for _ in scans]
        for lvl, mask in enumerate(masks):
            for n, (levels, _, _, q, k) in enumerate(gates):
                a = levels[lvl]
                scores[n] = jnp.where(mask, _dot_nt((q * a).astype(BF16), (k * a).astype(BF16)), scores[n])
        inter = [_dot_nt((q * p64).astype(BF16), st[n].astype(BF16)) for n, (_, p64, _, q, _) in enumerate(gates)]
        update = [_dot_tn(ci[n], (k * x64).astype(BF16)) for n, (_, _, x64, _, k) in enumerate(gates)]
        intra = [_dot(scores[n].astype(BF16), ci[n]) for n in range(len(scans))]
        for n, (d, h, cols, _, _, _, _, o_ref) in enumerate(scans):
            _, p64, _, q, k = gates[n]
            diag = jnp.sum(q * k, axis=-1, keepdims=True)
            o_ref[sl[d], cols] = (intra[n] + diag * ci[n].astype(F32) + inter[n]).astype(o_ref.dtype)
            st_sc[d, h] = st[n] * p64[C_CHUNK - 1:C_CHUNK] + update[n]
        return carry

    lax.fori_loop(0, n_chunks, body, 0, unroll=2)

    def flip_out(c, carry):
        ob_ref[rows(c), :] = _dot(flip, or_sc[rows(c), :]).astype(ob_ref.dtype)
        return carry

    lax.fori_loop(0, n_chunks, flip_out, 0, unroll=True)


def _hgrn(proj, lbf, lbb, groups, blk=512, heads=4):
    t = proj.shape[0]
    blk = min(blk, min(s for _, s in groups))
    lo, hi, _ = _seq_table(groups, blk)
    idx = np.arange(t // blk)
    reset = jnp.asarray((idx * blk == lo).astype(np.int32))
    rev = jnp.asarray((lo // blk + hi // blk - 1 - idx).astype(np.int32))
    width = heads * C_HEAD_DIM

    def fwd(start):
        return pl.BlockSpec((blk, width), lambda h, j, rs, rv: (j, start // width + h))

    def bwd(start):
        return pl.BlockSpec((blk, width), lambda h, j, rs, rv: (rv[j], start // width + h))

    lbs = pl.BlockSpec((1, width), lambda h, j, rs, rv: (0, h))
    grid_spec = pltpu.PrefetchScalarGridSpec(
        num_scalar_prefetch=2,
        grid=(C_HEADS // heads, t // blk),
        in_specs=[fwd(COL_CFF), fwd(COL_CQ), fwd(COL_CI), bwd(COL_CFB), bwd(COL_CQ), bwd(COL_CI), lbs, lbs],
        out_specs=[fwd(0), bwd(0)],
        scratch_shapes=[pltpu.VMEM((2, heads, C_HEAD_DIM, C_HEAD_DIM), F32),
                        pltpu.VMEM((2, heads, C_CHUNK, C_HEAD_DIM), F32),
                        pltpu.VMEM((2, heads, 5, C_CHUNK, C_HEAD_DIM), F32)]
        + [pltpu.VMEM((blk, width), BF16) for _ in range(4)],
    )
    return pl.pallas_call(
        functools.partial(_hgrn_kernel, n_chunks=blk // C_CHUNK, heads=heads),
        grid_spec=grid_spec,
        out_shape=[jax.ShapeDtypeStruct((t, 1024), BF16), jax.ShapeDtypeStruct((t, 1024), BF16)],
        compiler_params=_cparams(2),
    )(reset, rev, proj, proj, proj, proj, proj, proj, lbf, lbb)


def _merge_kernel(*refs, tile_starts):
    n_groups = len(tile_starts)
    x_ref = refs[0]
    oa_refs = refs[1:1 + n_groups]
    (ob0_ref, ob1_ref, ob2_ref, l0_ref, l1_ref, l2_ref, ocf_ref, ocb_ref,
     cg_ref, gates_ref, gn_ref, wa_ref, wb_ref, wc_ref, wm_ref, o_ref) = refs[1 + n_groups:]
    i = pl.program_id(0)
    oa = oa_refs[0][...]
    for g in range(1, n_groups):
        oa = jnp.where(i >= tile_starts[g], oa_refs[g][...], oa)
    l0, l1, l2 = l0_ref[...], l1_ref[...], l2_ref[...]
    m = jnp.maximum(jnp.maximum(l0, l1), l2)
    w0, w1, w2 = jnp.exp(l0 - m), jnp.exp(l1 - m), jnp.exp(l2 - m)
    ob = (w0 * ob0_ref[...].astype(F32) + w1 * ob1_ref[...].astype(F32)
          + w2 * ob2_ref[...].astype(F32)) / (w0 + w1 + w2)
    oc = ocf_ref[...].astype(F32) + ocb_ref[...].astype(F32)
    cg = cg_ref[...].astype(F32)
    ocn = _rms(oc, gn_ref[...]) * (cg * _sigmoid(cg))
    ya = _dot(oa, wa_ref[...])
    yb = _dot(ob.astype(BF16), wb_ref[...])
    yc = _dot(ocn.astype(BF16), wc_ref[...])
    gates = _sigmoid(gates_ref[...].astype(F32))
    merged = gates[:, :1024] * ya + gates[:, 1024:2048] * yb + gates[:, 2048:] * yc
    o_ref[...] = x_ref[...] + _dot(merged.astype(BF16), wm_ref[...])


def _merge(x, oas, obs, lses, ocf, ocb, proj, gn, wa, wb, wc, wm, tm=512):
    t = x.shape[0]
    tile_starts, off = [], 0
    for oa in oas:
        assert oa.shape[0] % tm == 0
        tile_starts.append(off)
        off += oa.shape[0] // tm

    def rows(width):
        return pl.BlockSpec((tm, width), lambda i: (i, 0))

    def full(shape):
        return pl.BlockSpec(shape, lambda i: (0, 0))

    def group_rows(start, n_tiles):
        return pl.BlockSpec((tm, 1024), lambda i: (jnp.clip(i - start, 0, n_tiles - 1), 0))

    oa_specs = [group_rows(start, oa.shape[0] // tm) for start, oa in zip(tile_starts, oas)]
    return pl.pallas_call(
        functools.partial(_merge_kernel, tile_starts=tuple(tile_starts)),
        grid=(t // tm,),
        in_specs=[rows(1024)] + oa_specs + [rows(256), rows(256), rows(256), rows(256), rows(256), rows(256),
                  rows(1024), rows(1024),
                  pl.BlockSpec((tm, 1024), lambda i: (i, COL_CG // 1024)),
                  pl.BlockSpec((tm, 3072), lambda i: (i, COL_GATES // 3072)),
                  full((1, 1024)), full((1024, 1024)), full((256, 1024)), full((1024, 1024)), full((1024, 1024))],
        out_specs=rows(1024),
        out_shape=jax.ShapeDtypeStruct((t, 1024), F32),
        compiler_params=_cparams(1),
    )(x, *oas, *obs, *lses, ocf, ocb, proj, proj, gn, wa, wb, wc, wm)


def _mem_kv_kernel(mem_ref, g_ref, w_ref, gk_ref, k_ref, v_ref):
    u = _rms(mem_ref[0], g_ref[...]).astype(BF16)
    kv = _dot(u, w_ref[...])
    for h in range(X_HEADS):
        sl = slice(h * X_HEAD_DIM, (h + 1) * X_HEAD_DIM)
        k_ref[0, :, sl] = _rms(kv[:, sl], gk_ref[...]).astype(BF16)
    v_ref[0] = kv[:, D_MODEL:].astype(BF16)


def _mem_kv(mem, g, wkv, gk):
    nb, m, d = mem.shape
    return pl.pallas_call(
        _mem_kv_kernel,
        grid=(nb,),
        in_specs=[pl.BlockSpec((1, m, d), lambda b: (b, 0, 0)),
                  pl.BlockSpec((1, d), lambda b: (0, 0)),
                  pl.BlockSpec((d, 2 * d), lambda b: (0, 0)),
                  pl.BlockSpec((1, X_HEAD_DIM), lambda b: (0, 0))],
        out_specs=[pl.BlockSpec((1, m, d), lambda b: (b, 0, 0)), pl.BlockSpec((1, m, d), lambda b: (b, 0, 0))],
        out_shape=[jax.ShapeDtypeStruct((nb, m, d), BF16), jax.ShapeDtypeStruct((nb, m, d), BF16)],
        compiler_params=_cparams(1),
    )(mem, g, wkv, gk)


def _cross_kernel(sid_ref, x_ref, g_ref, wq_ref, gq_ref, k_ref, v_ref, wo_ref, o_ref):
    del sid_ref
    x = x_ref[...]
    u = _rms(x, g_ref[...]).astype(BF16)
    q = _dot(u, wq_ref[...])
    scale = X_HEAD_DIM ** -0.5
    outs = []
    for h in range(X_HEADS):
        sl = slice(h * X_HEAD_DIM, (h + 1) * X_HEAD_DIM)
        qh = (_rms(q[:, sl], gq_ref[...]) * scale).astype(BF16)
        s = _dot_nt(qh, k_ref[0, :, sl])
        m = jnp.max(s, axis=-1, keepdims=True)
        e = jnp.exp(s - m)
        p = (e / jnp.sum(e, axis=-1, keepdims=True)).astype(BF16)
        outs.append(_dot(p, v_ref[0, :, sl]).astype(BF16))
    o = jnp.concatenate(outs, axis=1)
    o_ref[...] = x + _dot(o, wo_ref[...])


def _cross(x, g, wq, gq, k, v, wo, groups, tm=512):
    t, d = x.shape
    _, _, sid = _seq_table(groups, tm)
    m = k.shape[1]

    def full(shape):
        return pl.BlockSpec(shape, lambda i, s: (0, 0))

    grid_spec = pltpu.PrefetchScalarGridSpec(
        num_scalar_prefetch=1,
        grid=(t // tm,),
        in_specs=[pl.BlockSpec((tm, d), lambda i, s: (i, 0)), full((1, d)), full((d, d)), full((1, X_HEAD_DIM)),
                  pl.BlockSpec((1, m, d), lambda i, s: (s[i], 0, 0)),
                  pl.BlockSpec((1, m, d), lambda i, s: (s[i], 0, 0)),
                  full((d, d))],
        out_specs=pl.BlockSpec((tm, d), lambda i, s: (i, 0)),
    )
    return pl.pallas_call(
        _cross_kernel,
        grid_spec=grid_spec,
        out_shape=jax.ShapeDtypeStruct((t, d), F32),
        compiler_params=_cparams(1),
    )(jnp.asarray(sid), x, g, wq, gq, k, v, wo)


def _ffn_kernel(first_ref, last_ref, x_ref, xp_ref, xn_ref, g_ref, wup_ref, cw_ref, cb_ref, wdn_ref, o_ref, u_sc,
                *, tm):
    i = pl.program_id(0)
    x = x_ref[...]
    g = g_ref[...]
    u_sc[0:tm, :] = _rms(x, g).astype(BF16)
    u_sc[tm:tm + 8, :] = _rms(xp_ref[...], g).astype(BF16)
    u_sc[tm + 8:tm + 16, :] = _rms(xn_ref[...], g).astype(BF16)
    has_prev = (first_ref[i] == 0).astype(F32)
    has_next = (last_ref[i] == 0).astype(F32)
    row = lax.broadcasted_iota(jnp.int32, (tm, FF_CHUNK), 0)
    first = row == 0
    last = row == tm - 1
    u = u_sc[...]

    def up(col0):
        return _dot(u, wup_ref[:, col0:col0 + FF_CHUNK])

    def conv(h, col0):
        hm = h[:tm]
        h_prev = jnp.where(first, h[tm + 7:tm + 8] * has_prev, pltpu.roll(hm, 1, 0))
        h_next = jnp.where(last, h[tm + 8:tm + 9] * has_next, pltpu.roll(hm, tm - 1, 0))
        w = cw_ref[:, col0:col0 + FF_CHUNK]
        return h_prev * w[0:1] + hm * w[1:2] + h_next * w[2:3] + cb_ref[:, col0:col0 + FF_CHUNK]

    n_chunks = D_FF // FF_CHUNK
    acc = x
    h_next = (up(0), up(D_FF))
    for c in range(n_chunks):
        h_a, h_g = h_next
        if c + 1 < n_chunks:
            h_next = (up((c + 1) * FF_CHUNK), up(D_FF + (c + 1) * FF_CHUNK))
        a = conv(h_a, c * FF_CHUNK)
        gg = conv(h_g, D_FF + c * FF_CHUNK)
        act = (a * (gg * _sigmoid(gg))).astype(BF16)
        acc = acc + _dot(act, wdn_ref[c * FF_CHUNK:(c + 1) * FF_CHUNK, :])
    o_ref[...] = acc


def _ffn(x, g, wup, cw, cb, wdn, groups, tm=512):
    t, d = x.shape
    lo, hi, _ = _seq_table(groups, tm)
    start = np.arange(t // tm) * tm
    is_first = jnp.asarray((start == lo).astype(np.int32))
    is_last = jnp.asarray((start + tm == hi).astype(np.int32))
    r8 = tm // 8
    n8 = t // 8

    def full(shape):
        return pl.BlockSpec(shape, lambda i, a, b: (0, 0))

    grid_spec = pltpu.PrefetchScalarGridSpec(
        num_scalar_prefetch=2,
        grid=(t // tm,),
        in_specs=[pl.BlockSpec((tm, d), lambda i, a, b: (i, 0)),
                  pl.BlockSpec((8, d), lambda i, a, b: (jnp.maximum(i * r8 - 1, 0), 0)),
                  pl.BlockSpec((8, d), lambda i, a, b: (jnp.minimum((i + 1) * r8, n8 - 1), 0)),
                  full((1, d)), full((d, 2 * D_FF)), full((3, 2 * D_FF)), full((1, 2 * D_FF)), full((D_FF, d))],
        out_specs=pl.BlockSpec((tm, d), lambda i, a, b: (i, 0)),
        scratch_shapes=[pltpu.VMEM((tm + 16, d), BF16)],
    )
    return pl.pallas_call(
        functools.partial(_ffn_kernel, tm=tm),
        grid_spec=grid_spec,
        out_shape=jax.ShapeDtypeStruct((t, d), F32),
        compiler_params=_cparams(1),
    )(is_first, is_last, x, x, x, g, wup, cw, cb, wdn)


def _permute_w_in(w):
    aq, ak, av = w[:, 0:1024], w[:, 1024:1280], w[:, 1280:1536]
    bq, bk, bv = w[:, 1536:2304], w[:, 2304:3072], w[:, 3072:3840]
    cq, ci, cff, cfb, cg = (w[:, 3840 + 1024 * n:4864 + 1024 * n] for n in range(5))
    gates = w[:, 8960:12032]
    pad = jnp.zeros((w.shape[0], 256), w.dtype)
    return jnp.concatenate([gates, bq, bk, bv, ak, av, pad, aq, cq, ci, cff, cfb, cg], axis=1).astype(BF16)


def _lower_bound(raw, layer):
    p = jax.nn.softmax(raw.astype(F32), axis=0)
    return (jnp.cumsum(p, axis=0) - p[0])[layer][None, :]


def _trunk(x, mem, groups, p):
    depth = p['g_mix'].shape[0]
    smax = max(s for _, s in groups)
    tabs_a, tabs_b = _rope_tables(smax)
    for layer in range(depth):
        row = lambda name: p[name][layer][None, :].astype(F32)
        wbf = lambda name: p[name][layer].astype(BF16)
        proj = _norm_proj(x, row('g_mix'), _permute_w_in(p['w_in'][layer]))
        gbq = jnp.broadcast_to(p['b_gq'][layer][:, None, :], (3, B_HEADS, B_HEAD_DIM)).reshape(1, 768)
        gbk = jnp.broadcast_to(p['b_gk'][layer][:, None, :], (3, B_HEADS, B_HEAD_DIM)).reshape(1, 768)
        aq, ak, av, bq, bk = _qk_prep(proj, tabs_a, tabs_b, row('a_gq'), row('a_gk'),
                                      gbq.astype(F32), gbk.astype(F32), groups)
        oa = _attention_a(aq, ak, av, groups)
        obs, lses = [], []
        for grp, (_, dil) in enumerate(B_PATTERNS):
            o, lse = _band_attention(bq, bk, proj, grp, dil, groups)
            obs.append(o)
            lses.append(lse)
        ocf, ocb = _hgrn(proj, _lower_bound(p['c_lb_fwd'], layer), _lower_bound(p['c_lb_bwd'], layer), groups)
        x = _merge(x, oa, obs, lses, ocf, ocb, proj, row('c_gnorm'),
                   wbf('w_br_a'), wbf('w_br_b'), wbf('w_br_c'), wbf('w_mix_out'))
        mk, mv = _mem_kv(mem, row('g_mem'), wbf('x_wkv'), row('x_gk'))
        x = _cross(x, row('g_cross'), wbf('x_wq'), row('x_gq'), mk, mv, wbf('x_wo'), groups)
        x = _ffn(x, row('g_ffn'), wbf('f_wup'), p['f_conv'][layer].astype(F32), row('f_conv_b'),
                 wbf('f_wdown'), groups)
    return x


def kernel(x_prompt, x_sample, mem_prompt, mem_sample, g_mix, w_in, a_gq, a_gk, b_gq, b_gk, c_lb_fwd, c_lb_bwd, c_gnorm, w_br_a, w_br_b, w_br_c, w_mix_out, g_cross, g_mem, x_wq, x_wkv, x_gq, x_gk, x_wo, g_ffn, f_wup, f_conv, f_conv_b, f_wdown):
    p = dict(g_mix=g_mix, w_in=w_in, a_gq=a_gq, a_gk=a_gk, b_gq=b_gq, b_gk=b_gk,
             c_lb_fwd=c_lb_fwd, c_lb_bwd=c_lb_bwd, c_gnorm=c_gnorm, w_br_a=w_br_a, w_br_b=w_br_b,
             w_br_c=w_br_c, w_mix_out=w_mix_out, g_cross=g_cross, g_mem=g_mem, x_wq=x_wq,
             x_wkv=x_wkv, x_gq=x_gq, x_gk=x_gk, x_wo=x_wo, g_ffn=g_ffn, f_wup=f_wup,
             f_conv=f_conv, f_conv_b=f_conv_b, f_wdown=f_wdown)
    groups = [(x_prompt.shape[0], x_prompt.shape[1]), (x_sample.shape[0], x_sample.shape[1])]
    d = x_prompt.shape[-1]
    x = jnp.concatenate([x_prompt.reshape(-1, d), x_sample.reshape(-1, d)], axis=0)
    mem = jnp.concatenate([mem_prompt, mem_sample], axis=0)
    y = _trunk(x, mem, groups, p)
    n_prompt = x_prompt.shape[0] * x_prompt.shape[1]
    return (y[:n_prompt].reshape(x_prompt.shape), y[n_prompt:].reshape(x_sample.shape))
```

```python
import functools

import numpy as np
import jax
import jax.numpy as jnp
from jax import lax
from jax.experimental import pallas as pl
from jax.experimental.pallas import tpu as pltpu

F32 = jnp.float32
BF16 = jnp.bfloat16

D_MODEL = 1024
GRID_W = 64
EPS = 1e-6
ROPE_THETA = 10000.0
NEG_BIG = -1e30
GATE_CLIP = 30.0
LOG2E = 1.4426950408889634
BOUND_SLACK = 1.0 + 2.0 ** -6
ROW_SUM_FLOOR = 2.0 ** -100

A_HEADS = 8
A_KV_HEADS = 2
A_HEAD_DIM = 128
A_GROUP = A_HEADS // A_KV_HEADS

B_PATTERNS = ((128, 1), (512, 4), (2048, 16))
B_HEADS = 4
B_HEAD_DIM = 64
B_GROUP_WIDTH = B_HEADS * B_HEAD_DIM
B_SIDE = 64
V_AUG_ROWS = A_HEAD_DIM + 16

C_HEADS = 8
C_HEAD_DIM = 128
C_CHUNK = 64

X_HEADS = 4
X_HEAD_DIM = D_MODEL // X_HEADS

D_FF = 2816
FF_CHUNK = 256

COL_GATES = 0
COL_BQ = 3072
COL_BK = 3840
COL_BV = 4608
COL_AK = 5376
COL_AV = 5632
COL_AQ = 6144
COL_CQ = 7168
COL_CI = 8192
COL_CFF = 9216
COL_CFB = 10240
COL_CG = 11264
PROJ_WIDTH = 12288

VMEM_LIMIT_BYTES = 56 * 1024 * 1024


def _cparams(n_axes, vmem=VMEM_LIMIT_BYTES):
    return pltpu.CompilerParams(dimension_semantics=("arbitrary",) * n_axes, vmem_limit_bytes=vmem)


def _dot(a, b):
    return jnp.dot(a, b, preferred_element_type=F32)


def _dot_nt(a, b):
    return lax.dot_general(a, b, (((1,), (1,)), ((), ())), preferred_element_type=F32)


def _dot_tn(a, b):
    return lax.dot_general(a, b, (((0,), (0,)), ((), ())), preferred_element_type=F32)


def _rms(x, g):
    ms = jnp.mean(x * x, axis=-1, keepdims=True)
    return x * lax.rsqrt(ms + EPS) * g


def _sigmoid(x):
    return 1.0 / (1.0 + jnp.exp(-x))


def _seq_table(groups, tile):
    lo, hi, sid = [], [], []
    off, s_idx = 0, 0
    for (b, s) in groups:
        assert s % tile == 0
        for _ in range(b):
            for _ in range(s // tile):
                lo.append(off)
                hi.append(off + s)
                sid.append(s_idx)
            off += s
            s_idx += 1
    return np.asarray(lo, np.int32), np.asarray(hi, np.int32), np.asarray(sid, np.int32)


def _proj_kernel(x_ref, g_ref, w_ref, o_ref, u_sc):
    @pl.when(pl.program_id(1) == 0)
    def _():
        u_sc[...] = _rms(x_ref[...], g_ref[...]).astype(BF16)

    o_ref[...] = _dot(u_sc[...], w_ref[...]).astype(o_ref.dtype)


def _norm_proj(x, g, w, tm=1024, tn=1024):
    t, k = x.shape
    n = w.shape[1]
    return pl.pallas_call(
        _proj_kernel,
        grid=(t // tm, n // tn),
        in_specs=[
            pl.BlockSpec((tm, k), lambda i, j: (i, 0)),
            pl.BlockSpec((1, k), lambda i, j: (0, 0)),
            pl.BlockSpec((k, tn), lambda i, j: (0, j)),
        ],
        out_specs=pl.BlockSpec((tm, tn), lambda i, j: (i, j)),
        out_shape=jax.ShapeDtypeStruct((t, n), BF16),
        scratch_shapes=[pltpu.VMEM((tm, k), BF16)],
        compiler_params=_cparams(2),
    )(x, g, w)


def _rope(x, c, s1, s2):
    return x * c + pltpu.roll(x, 96, 1) * s1 + pltpu.roll(x, 32, 1) * s2


def _prep_kernel(pos_ref, aq_ref, ak_ref, av_ref, bq_ref, bk_ref,
                 ca_ref, s1a_ref, s2a_ref, cb_ref, s1b_ref, s2b_ref,
                 gaq_ref, gak_ref, gbq_ref, gbk_ref, bd_ref,
                 oaq_ref, oak_ref, oav_ref, obq_ref, obk_ref):
    del pos_ref
    ca, s1a, s2a = ca_ref[...], s1a_ref[...], s2a_ref[...]
    cb, s1b, s2b = cb_ref[...], s1b_ref[...], s2b_ref[...]
    a_scale = A_HEAD_DIM ** -0.5 * LOG2E
    b_scale = B_HEAD_DIM ** -0.5
    tm = aq_ref.shape[0]
    lane0 = lax.broadcasted_iota(jnp.int32, (tm, 128), 1) == 0
    row0 = lax.broadcasted_iota(jnp.int32, (128, tm), 0) == 0

    def a_head(x, g, scale):
        y = _rms(x.astype(F32), g)
        return _rope(y, ca, s1a, s2a) * scale

    k_bound = jnp.max(jnp.abs(gak_ref[...]), axis=-1, keepdims=True) * (A_HEAD_DIM ** 0.5 * BOUND_SLACK)
    for h in range(A_HEADS):
        sl = slice(h * 128, (h + 1) * 128)
        q_t = a_head(aq_ref[:, sl], gaq_ref[...], a_scale).T
        bound_t = jnp.sqrt(jnp.sum(q_t * q_t, axis=0, keepdims=True)) * k_bound
        qcols = slice((h % A_GROUP) * tm, (h % A_GROUP + 1) * tm)
        oaq_ref[h // A_GROUP, 0:128, qcols] = q_t.astype(BF16)
        oaq_ref[h // A_GROUP, 128:256, qcols] = jnp.where(row0, -bound_t, 0.0).astype(BF16)
    for h in range(A_KV_HEADS):
        sl = slice(h * 128, (h + 1) * 128)
        oak_ref[:, h * 256:h * 256 + 128] = a_head(ak_ref[:, sl], gak_ref[...], 1.0).astype(BF16)
        oak_ref[:, h * 256 + 128:(h + 1) * 256] = jnp.where(lane0, 1.0, 0.0).astype(BF16)
        oav_ref[h, 0, 0:128, :] = av_ref[:, sl].astype(F32).T.astype(BF16)
        ones_row = lax.broadcasted_iota(jnp.int32, (V_AUG_ROWS - 128, tm), 0) == 0
        oav_ref[h, 0, 128:V_AUG_ROWS, :] = jnp.where(ones_row, 1.0, 0.0).astype(BF16)

    bd = bd_ref[...]

    def b_group(x_ref, g_ref, o_ref, scale):
        for grp in range(3):
            sl = slice(grp * 256, (grp + 1) * 256)
            x = x_ref[:, sl].astype(F32)
            ms = _dot((x * x).astype(BF16), bd)
            y = x * lax.rsqrt(ms + EPS) * g_ref[:, sl]
            for half in range(2):
                hs = slice(half * 128, (half + 1) * 128)
                o_ref[:, grp * 256 + half * 128:grp * 256 + (half + 1) * 128] = (
                    _rope(y[:, hs], cb, s1b, s2b) * scale).astype(BF16)

    b_group(bq_ref, gbq_ref, obq_ref, b_scale)
    b_group(bk_ref, gbk_ref, obk_ref, 1.0)


def _rope_tables(smax):
    half = A_HEAD_DIM // 2
    inv = jnp.power(ROPE_THETA, -(jnp.arange(0, half, 2, dtype=F32) / half))
    t = jnp.arange(smax)
    lane = jnp.arange(128)
    first = (lane % 64) < 32

    def tables(ang):
        c = jnp.cos(ang)
        s = jnp.sin(ang)
        return c, jnp.where(first[None, :], -s, 0.0), jnp.where(first[None, :], 0.0, s)

    inv128 = jnp.tile(inv, 4)[None, :]
    row = (t // GRID_W).astype(F32)[:, None]
    col = (t % GRID_W).astype(F32)[:, None]
    ang_a = jnp.where((lane < 64)[None, :], row * inv128, col * inv128)
    ang_b = t.astype(F32)[:, None] * inv128
    return tables(ang_a), tables(ang_b)


def _qk_prep(proj, tabs_a, tabs_b, gaq, gak, gbq, gbk, groups, tm=512):
    t = proj.shape[0]
    lo, _, _ = _seq_table(groups, tm)
    pos_blk = jnp.asarray((np.arange(t // tm) * tm - lo) // tm, jnp.int32)
    bd = jnp.asarray(np.kron(np.eye(4), np.full((64, 64), 1.0 / 64)), BF16)

    def col(width, start):
        return pl.BlockSpec((tm, width), lambda i, p: (i, start // width))

    tab = pl.BlockSpec((tm, 128), lambda i, p: (p[i], 0))

    def full(shape):
        return pl.BlockSpec(shape, lambda i, p: (0, 0))

    def out(width):
        return pl.BlockSpec((tm, width), lambda i, p: (i, 0))

    grid_spec = pltpu.PrefetchScalarGridSpec(
        num_scalar_prefetch=1,
        grid=(t // tm,),
        in_specs=[col(1024, COL_AQ), col(256, COL_AK), col(256, COL_AV), col(768, COL_BQ), col(768, COL_BK),
                  tab, tab, tab, tab, tab, tab,
                  full((1, 128)), full((1, 128)), full((1, 768)), full((1, 768)), full((256, 256))],
        out_specs=[pl.BlockSpec((A_KV_HEADS, 256, A_GROUP * tm), lambda i, p: (0, 0, i)),
                   out(512),
                   pl.BlockSpec((A_KV_HEADS, 1, V_AUG_ROWS, tm), lambda i, p: (0, i, 0, 0)),
                   out(768), out(768)],
    )
    return pl.pallas_call(
        _prep_kernel,
        grid_spec=grid_spec,
        out_shape=[jax.ShapeDtypeStruct((A_KV_HEADS, 256, A_GROUP * t), BF16),
                   jax.ShapeDtypeStruct((t, 512), BF16),
                   jax.ShapeDtypeStruct((A_KV_HEADS, t // tm, V_AUG_ROWS, tm), BF16),
                   jax.ShapeDtypeStruct((t, 768), BF16), jax.ShapeDtypeStruct((t, 768), BF16)],
        compiler_params=_cparams(1),
    )(pos_blk, proj, proj, proj, proj, proj, *tabs_a, *tabs_b, gaq, gak, gbq, gbk, bd)


def _flash_kernel(qt_ref, k_ref, vt_ref, o_ref, acc_sc, m_sc, *, tq, tk, nk):
    def scores_t(c):
        return _dot(k_ref[pl.ds(pl.multiple_of(c * tk, tk), tk), :], qt_ref[...])

    def accumulate(c, carry):
        p = jnp.exp2(scores_t(c)).astype(BF16)
        acc_sc[...] += _dot(vt_ref[c], p)
        return carry

    acc_sc[...] = jnp.zeros(acc_sc.shape, F32)
    lax.fori_loop(0, nk, accumulate, 0, unroll=4 if nk % 4 == 0 else 1)

    @pl.when(jnp.min(acc_sc[128:129, :]) < ROW_SUM_FLOOR)
    def _():
        def col_max(c, carry):
            m_sc[...] = jnp.maximum(m_sc[...], jnp.max(scores_t(c), axis=0, keepdims=True))
            return carry

        def accumulate_exact(c, carry):
            p = jnp.exp2(scores_t(c) - m_sc[0:1, :]).astype(BF16)
            acc_sc[...] += _dot(vt_ref[c], p)
            return carry

        m_sc[...] = jnp.full(m_sc.shape, -jnp.inf, F32)
        lax.fori_loop(0, nk, col_max, 0)
        acc_sc[...] = jnp.zeros(acc_sc.shape, F32)
        lax.fori_loop(0, nk, accumulate_exact, 0)

    out_t = acc_sc[0:128, :] / acc_sc[128:129, :]
    for h in range(A_GROUP):
        o_ref[:, h * 128:(h + 1) * 128] = out_t[:, h * tq:(h + 1) * tq].T.astype(o_ref.dtype)


def _flash_group(aq, ak, av, tok_off, b, s, tq, tk):
    assert tok_off % s == 0 and s % tq == 0 and s % tk == 0 and av.shape[-1] == tk
    nq = s // tq
    qoff, soff = tok_off // tq, tok_off // s
    in_specs = [
        pl.BlockSpec((None, 256, A_GROUP * tq), lambda bb, g, i: (g, 0, qoff + bb * nq + i)),
        pl.BlockSpec((s, 256), lambda bb, g, i: (soff + bb, g), pipeline_mode=pl.Buffered(1)),
        pl.BlockSpec((None, s // tk, V_AUG_ROWS, tk), lambda bb, g, i: (g, soff + bb, 0, 0),
                     pipeline_mode=pl.Buffered(1)),
    ]
    return pl.pallas_call(
        functools.partial(_flash_kernel, tq=tq, tk=tk, nk=s // tk),
        grid=(b, A_KV_HEADS, nq),
        in_specs=in_specs,
        out_specs=pl.BlockSpec((tq, 512), lambda bb, g, i: (bb * nq + i, g)),
        out_shape=jax.ShapeDtypeStruct((b * s, 1024), BF16),
        scratch_shapes=[pltpu.VMEM((V_AUG_ROWS, A_GROUP * tq), F32), pltpu.VMEM((8, A_GROUP * tq), F32)],
        compiler_params=_cparams(3),
    )(aq, ak, av)


def _attention_a(aq, ak, av, groups, tq=512, tk=512):
    outs = []
    off = 0
    for (b, s) in groups:
        outs.append(_flash_group(aq, ak, av, off, b, s, tq, tk))
        off += b * s
    return outs


def _band_tiles(tiles):
    tq = tiles[0][0].shape[0]
    nk = tq + 2 * B_SIDE
    rel = (lax.broadcasted_iota(jnp.int32, (tq, nk), 1) - B_SIDE) - lax.broadcasted_iota(jnp.int32, (tq, nk), 0)
    band = jnp.abs(rel) <= B_SIDE
    head_of_lane = lax.broadcasted_iota(jnp.int32, (1, B_GROUP_WIDTH), 1) // B_HEAD_DIM
    hms = [head_of_lane == h for h in range(B_HEADS)]
    ss = [[_dot_nt(jnp.where(hm, q, jnp.zeros_like(q)), k) for hm in hms] for (q, k, _, _) in tiles]
    ps, lses = [], []
    for (_, _, _, valid), s_heads in zip(tiles, ss):
        mask = band & valid
        p_heads, l_heads = [], []
        for s in s_heads:
            s = jnp.where(mask, s, NEG_BIG)
            m = jnp.max(s, axis=-1, keepdims=True)
            e = jnp.where(mask, jnp.exp(s - m), 0.0)
            den = jnp.sum(e, axis=-1, keepdims=True)
            p_heads.append((e / den).astype(BF16))
            l_heads.append(m + jnp.log(den))
        ps.append(p_heads)
        lses.append(l_heads)
    os_ = [[_dot(p, v) for p in p_heads] for (_, _, v, _), p_heads in zip(tiles, ps)]
    results = []
    for o_heads, l_heads in zip(os_, lses):
        out = jnp.zeros((tq, B_GROUP_WIDTH), F32)
        lse_out = jnp.zeros((tq, B_GROUP_WIDTH), F32)
        for hm, o, lse in zip(hms, o_heads, l_heads):
            out = jnp.where(hm, o, out)
            lse_out = jnp.where(hm, lse, lse_out)
        results.append((out, lse_out))
    return results


def _band_kernel(lo_ref, hi_ref, q_ref, kp_ref, kc_ref, kn_ref, vp_ref, vc_ref, vn_ref, o_ref, lse_ref,
                 *scratch, tq, dil):
    i = pl.program_id(0)
    span = tq * dil
    halo = B_SIDE * dil
    nk = tq + 2 * B_SIDE
    t0 = i * span
    lo = lo_ref[i]
    hi = hi_ref[i]
    col = lax.broadcasted_iota(jnp.int32, (tq, nk), 1)

    if dil == 1:
        k = jnp.concatenate([kp_ref[...], kc_ref[...], kn_ref[...]], axis=0)
        v = jnp.concatenate([vp_ref[...], vc_ref[...], vn_ref[...]], axis=0)
        kpos = t0 - halo + col
        (out, lse), = _band_tiles([(q_ref[...], k, v, (kpos >= lo) & (kpos < hi))])
        o_ref[...] = out.astype(o_ref.dtype)
        lse_ref[...] = lse
        return

    qn_sc, kn_sc, vn_sc, on_sc, ln_sc = scratch
    def stage(dst, row0, src_ref):
        x = src_ref[...].astype(F32)
        n = x.shape[0]
        dst[0, row0:row0 + n, :] = x[:, :128]
        dst[1, row0:row0 + n, :] = x[:, 128:]

    def pick(src, r, n):
        rows = pl.ds(r, n, stride=dil)
        return jnp.concatenate([src[0, rows, :], src[1, rows, :]], axis=1).astype(BF16)

    def put(dst, r, x):
        rows = pl.ds(r, tq, stride=dil)
        dst[0, rows, :] = x[:, :128]
        dst[1, rows, :] = x[:, 128:]

    stage(qn_sc, 0, q_ref)
    for dst, (p_ref, c_ref, n_ref) in ((kn_sc, (kp_ref, kc_ref, kn_ref)), (vn_sc, (vp_ref, vc_ref, vn_ref))):
        stage(dst, 0, p_ref)
        stage(dst, halo, c_ref)
        stage(dst, halo + span, n_ref)

    def residue_pair(r0, carry):
        rs = (r0, r0 + dil // 2)
        tiles = []
        for r in rs:
            kpos = t0 - halo + r + dil * col
            tiles.append((pick(qn_sc, r, tq), pick(kn_sc, r, nk), pick(vn_sc, r, nk), (kpos >= lo) & (kpos < hi)))
        for r, (out, lse) in zip(rs, _band_tiles(tiles)):
            put(on_sc, r, out)
            put(ln_sc, r, lse)
        return carry

    lax.fori_loop(0, dil // 2, residue_pair, 0)
    for half in range(2):
        lanes = slice(half * 128, (half + 1) * 128)
        o_ref[:, lanes] = on_sc[half].astype(o_ref.dtype)
        lse_ref[:, lanes] = ln_sc[half]


def _band_attention(bq, bk, proj, grp, dil, groups, tq=128):
    t = bq.shape[0]
    span = tq * dil
    halo = B_SIDE * dil
    lo, hi, _ = _seq_table(groups, span)
    r_halo = span // halo
    n_halo = t // halo
    vblk = COL_BV // 256 + grp

    def blk(rows, row_fn, colblk):
        return pl.BlockSpec((rows, 256), lambda i, lo_, hi_: (row_fn(i), colblk))

    prev_fn = lambda i: jnp.maximum(i * r_halo - 1, 0)
    next_fn = lambda i: jnp.minimum((i + 1) * r_halo, n_halo - 1)
    cur_fn = lambda i: i
    scratch = []
    if dil > 1:
        scratch = [pltpu.VMEM((2, span, 128), F32), pltpu.VMEM((2, span + 2 * halo, 128), F32),
                   pltpu.VMEM((2, span + 2 * halo, 128), F32), pltpu.VMEM((2, span, 128), F32),
                   pltpu.VMEM((2, span, 128), F32)]
    grid_spec = pltpu.PrefetchScalarGridSpec(
        num_scalar_prefetch=2,
        grid=(t // span,),
        in_specs=[blk(span, cur_fn, grp), blk(halo, prev_fn, grp), blk(span, cur_fn, grp), blk(halo, next_fn, grp),
                  blk(halo, prev_fn, vblk), blk(span, cur_fn, vblk), blk(halo, next_fn, vblk)],
        out_specs=[blk(span, cur_fn, 0), blk(span, cur_fn, 0)],
        scratch_shapes=scratch,
    )
    return pl.pallas_call(
        functools.partial(_band_kernel, tq=tq, dil=dil),
        grid_spec=grid_spec,
        out_shape=[jax.ShapeDtypeStruct((t, 256), BF16), jax.ShapeDtypeStruct((t, 256), F32)],
        compiler_params=_cparams(1),
    )(jnp.asarray(lo), jnp.asarray(hi), bq, bk, bk, bk, proj, proj, proj)


def _segment_products(f, f_sc, a_sc):
    f_sc[...] = f
    f8 = [f_sc[pl.ds(j, 8, stride=8), :] for j in range(8)]
    one = jnp.ones_like(f8[0])
    pair = [f8[j] * f8[j + 1] for j in (0, 2, 4, 6)]
    lo4, hi4 = pair[0] * pair[1], pair[2] * pair[3]
    a2 = [one, f8[1], one, f8[3], one, f8[5], one, f8[7]]
    a4 = [f8[1], one, f8[2], pair[1], f8[5], one, f8[6], pair[3]]
    a8 = [f8[1] * pair[1], pair[1], f8[3], one, f8[4], pair[2], pair[2] * f8[6], hi4]
    p8 = [f8[0], pair[0], pair[0] * f8[2], lo4, lo4 * f8[4], lo4 * pair[2], lo4 * (pair[2] * f8[6]), lo4 * hi4]
    x8 = [a8[0] * hi4, pair[1] * hi4, f8[3] * hi4, hi4, f8[5] * pair[3], pair[3], f8[7], one]
    for slot, arr in enumerate((a2, a4, a8, p8, x8)):
        for j in range(8):
            a_sc[slot, pl.ds(j, 8, stride=8), :] = arr[j]
    out = [a_sc[0], a_sc[1], a_sc[2]]
    p, x = a_sc[3], a_sc[4]
    for n in (8, 16, 32):
        n2 = 2 * n
        ap, pp, xp = [], [], []
        for s0 in range(0, C_CHUNK, n2):
            tot_l = p[s0 + n - 1:s0 + n]
            tot_r = p[s0 + n2 - 1:s0 + n2]
            ap += [x[s0:s0 + n], p[s0 + n:s0 + n2]]
            pp += [p[s0:s0 + n], p[s0 + n:s0 + n2] * tot_l]
            xp += [x[s0:s0 + n] * tot_r, x[s0 + n:s0 + n2]]
        out.append(jnp.concatenate(ap, axis=0))
        p = jnp.concatenate(pp, axis=0)
        x = jnp.concatenate(xp, axis=0)
    return out, p, x


def _level_masks():
    tt = lax.broadcasted_iota(jnp.int32, (C_CHUNK, C_CHUNK), 0)
    ss = lax.broadcasted_iota(jnp.int32, (C_CHUNK, C_CHUNK), 1)
    masks = []
    for shift in range(1, 7):
        n2 = 1 << shift
        n = n2 // 2
        masks.append(((tt >> shift) == (ss >> shift)) & ((tt & (n2 - 1)) >= n) & ((ss & (n2 - 1)) < n))
    return masks


def _hgrn_gates(z, cq, lb, f_sc, a_sc):
    zc = jnp.clip(z.astype(F32), -GATE_CLIP, GATE_CLIP)
    e = jnp.exp(-zc)
    sg = 1.0 / (1.0 + e)
    f = lb + (1.0 - lb) * sg
    k = (1.0 - lb) * (e * sg)
    cqf = cq.astype(F32)
    q = cqf * _sigmoid(cqf)
    levels, p64, x64 = _segment_products(f, f_sc, a_sc)
    return levels, p64, x64, q, k


def _hgrn_kernel(reset_ref, rev_ref, zf_ref, qf_ref, if_ref, zb_ref, qb_ref, ib_ref, lbf_ref, lbb_ref,
                 of_ref, ob_ref, st_sc, f_sc, a_sc, zr_sc, qr_sc, ir_sc, or_sc, *, n_chunks, heads):
    del rev_ref
    jblk = pl.program_id(1)

    @pl.when(reset_ref[jblk] == 1)
    def _():
        st_sc[...] = jnp.zeros(st_sc.shape, F32)

    rr = lax.broadcasted_iota(jnp.int32, (C_CHUNK, C_CHUNK), 0)
    cc = lax.broadcasted_iota(jnp.int32, (C_CHUNK, C_CHUNK), 1)
    flip = jnp.where(rr + cc == C_CHUNK - 1, 1.0, 0.0).astype(BF16)
    masks = _level_masks()

    def rows(c):
        return pl.ds(pl.multiple_of(c * C_CHUNK, C_CHUNK), C_CHUNK)

    def flip_in(c, carry):
        zr_sc[rows(c), :] = _dot(flip, zb_ref[rows(c), :]).astype(BF16)
        qr_sc[rows(c), :] = _dot(flip, qb_ref[rows(c), :]).astype(BF16)
        ir_sc[rows(c), :] = _dot(flip, ib_ref[rows(c), :]).astype(BF16)
        return carry

    lax.fori_loop(0, n_chunks, flip_in, 0, unroll=True)

    scans = []
    for h in range(heads):
        cols = slice(h * C_HEAD_DIM, (h + 1) * C_HEAD_DIM)
        scans.append((0, h, cols, zf_ref, qf_ref, if_ref, lbf_ref[:, cols], of_ref))
        scans.append((1, h, cols, zr_sc, qr_sc, ir_sc, lbb_ref[:, cols], or_sc))

    def body(c, carry):
        sl = (rows(c), rows(n_chunks - 1 - c))
        gates = [_hgrn_gates(z_ref[sl[d], cols], q_ref[sl[d], cols], lb, f_sc.at[d, h], a_sc.at[d, h])
                 for (d, h, cols, z_ref, q_ref, _, lb, _) in scans]
        ci = [i_ref[sl[d], cols] for (d, _, cols, _, _, i_ref, _, _) in scans]
        st = [st_sc[d, h] for (d, h, *_) in scans]
        scores = [jnp.zeros((C_CHUNK, C_CHUNK), F32) for _ in scans]
        for lvl, mask in enumerate(masks):
            for n, (levels, _, _, q, k) in enumerate(gates):
                a = levels[lvl]
                scores[n] = jnp.where(mask, _dot_nt((q * a).astype(BF16), (k * a).astype(BF16)), scores[n])
        inter = [_dot_nt((q * p64).astype(BF16), st[n].astype(BF16)) for n, (_, p64, _, q, _) in enumerate(gates)]
        update = [_dot_tn(ci[n], (k * x64).astype(BF16)) for n, (_, _, x64, _, k) in enumerate(gates)]
        intra = [_dot(scores[n].astype(BF16), ci[n]) for n in range(len(scans))]
        for n, (d, h, cols, _, _, _, _, o_ref) in enumerate(scans):
            _, p64, _, q, k = gates[n]
            diag = jnp.sum(q * k, axis=-1, keepdims=True)
            o_ref[sl[d], cols] = (intra[n] + diag * ci[n].astype(F32) + inter[n]).astype(o_ref.dtype)
            st_sc[d, h] = st[n] * p64[C_CHUNK - 1:C_CHUNK] + update[n]
        return carry

    lax.fori_loop(0, n_chunks, body, 0, unroll=2)

    def flip_out(c, carry):
        ob_ref[rows(c), :] = _dot(flip, or_sc[rows(c), :]).astype(ob_ref.dtype)
        return carry

    lax.fori_loop(0, n_chunks, flip_out, 0, unroll=True)


def _hgrn(proj, lbf, lbb, groups, blk=512, heads=4):
    t = proj.shape[0]
    blk = min(blk, min(s for _, s in groups))
    lo, hi, _ = _seq_table(groups, blk)
    idx = np.arange(t // blk)
    reset = jnp.asarray((idx * blk == lo).astype(np.int32))
    rev = jnp.asarray((lo // blk + hi // blk - 1 - idx).astype(np.int32))
    width = heads * C_HEAD_DIM

    def fwd(start):
        return pl.BlockSpec((blk, width), lambda h, j, rs, rv: (j, start // width + h))

    def bwd(start):
        return pl.BlockSpec((blk, width), lambda h, j, rs, rv: (rv[j], start // width + h))

    lbs = pl.BlockSpec((1, width), lambda h, j, rs, rv: (0, h))
    grid_spec = pltpu.PrefetchScalarGridSpec(
        num_scalar_prefetch=2,
        grid=(C_HEADS // heads, t // blk),
        in_specs=[fwd(COL_CFF), fwd(COL_CQ), fwd(COL_CI), bwd(COL_CFB), bwd(COL_CQ), bwd(COL_CI), lbs, lbs],
        out_specs=[fwd(0), bwd(0)],
        scratch_shapes=[pltpu.VMEM((2, heads, C_HEAD_DIM, C_HEAD_DIM), F32),
                        pltpu.VMEM((2, heads, C_CHUNK, C_HEAD_DIM), F32),
                        pltpu.VMEM((2, heads, 5, C_CHUNK, C_HEAD_DIM), F32)]
        + [pltpu.VMEM((blk, width), BF16) for _ in range(4)],
    )
    return pl.pallas_call(
        functools.partial(_hgrn_kernel, n_chunks=blk // C_CHUNK, heads=heads),
        grid_spec=grid_spec,
        out_shape=[jax.ShapeDtypeStruct((t, 1024), BF16), jax.ShapeDtypeStruct((t, 1024), BF16)],
        compiler_params=_cparams(2),
    )(reset, rev, proj, proj, proj, proj, proj, proj, lbf, lbb)


def _merge_kernel(*refs, tile_starts):
    n_groups = len(tile_starts)
    x_ref = refs[0]
    oa_refs = refs[1:1 + n_groups]
    (ob0_ref, ob1_ref, ob2_ref, l0_ref, l1_ref, l2_ref, ocf_ref, ocb_ref,
     cg_ref, gates_ref, gn_ref, wa_ref, wb_ref, wc_ref, wm_ref, o_ref) = refs[1 + n_groups:]
    i = pl.program_id(0)
    oa = oa_refs[0][...]
    for g in range(1, n_groups):
        oa = jnp.where(i >= tile_starts[g], oa_refs[g][...], oa)
    l0, l1, l2 = l0_ref[...], l1_ref[...], l2_ref[...]
    m = jnp.maximum(jnp.maximum(l0, l1), l2)
    w0, w1, w2 = jnp.exp(l0 - m), jnp.exp(l1 - m), jnp.exp(l2 - m)
    ob = (w0 * ob0_ref[...].astype(F32) + w1 * ob1_ref[...].astype(F32)
          + w2 * ob2_ref[...].astype(F32)) / (w0 + w1 + w2)
    oc = ocf_ref[...].astype(F32) + ocb_ref[...].astype(F32)
    cg = cg_ref[...].astype(F32)
    ocn = _rms(oc, gn_ref[...]) * (cg * _sigmoid(cg))
    ya = _dot(oa, wa_ref[...])
    yb = _dot(ob.astype(BF16), wb_ref[...])
    yc = _dot(ocn.astype(BF16), wc_ref[...])
    gates = _sigmoid(gates_ref[...].astype(F32))
    merged = gates[:, :1024] * ya + gates[:, 1024:2048] * yb + gates[:, 2048:] * yc
    o_ref[...] = x_ref[...] + _dot(merged.astype(BF16), wm_ref[...])


def _merge(x, oas, obs, lses, ocf, ocb, proj, gn, wa, wb, wc, wm, tm=512):
    t = x.shape[0]
    tile_starts, off = [], 0
    for oa in oas:
        assert oa.shape[0] % tm == 0
        tile_starts.append(off)
        off += oa.shape[0] // tm

    def rows(width):
        return pl.BlockSpec((tm, width), lambda i: (i, 0))

    def full(shape):
        return pl.BlockSpec(shape, lambda i: (0, 0))

    def group_rows(start, n_tiles):
        return pl.BlockSpec((tm, 1024), lambda i: (jnp.clip(i - start, 0, n_tiles - 1), 0))

    oa_specs = [group_rows(start, oa.shape[0] // tm) for start, oa in zip(tile_starts, oas)]
    return pl.pallas_call(
        functools.partial(_merge_kernel, tile_starts=tuple(tile_starts)),
        grid=(t // tm,),
        in_specs=[rows(1024)] + oa_specs + [rows(256), rows(256), rows(256), rows(256), rows(256), rows(256),
                  rows(1024), rows(1024),
                  pl.BlockSpec((tm, 1024), lambda i: (i, COL_CG // 1024)),
                  pl.BlockSpec((tm, 3072), lambda i: (i, COL_GATES // 3072)),
                  full((1, 1024)), full((1024, 1024)), full((256, 1024)), full((1024, 1024)), full((1024, 1024))],
        out_specs=rows(1024),
        out_shape=jax.ShapeDtypeStruct((t, 1024), F32),
        compiler_params=_cparams(1),
    )(x, *oas, *obs, *lses, ocf, ocb, proj, proj, gn, wa, wb, wc, wm)


def _mem_kv_kernel(mem_ref, g_ref, w_ref, gk_ref, k_ref, v_ref):
    u = _rms(mem_ref[0], g_ref[...]).astype(BF16)
    kv = _dot(u, w_ref[...])
    for h in range(X_HEADS):
        sl = slice(h * X_HEAD_DIM, (h + 1) * X_HEAD_DIM)
        k_ref[0, :, sl] = _rms(kv[:, sl], gk_ref[...]).astype(BF16)
    v_ref[0] = kv[:, D_MODEL:].astype(BF16)


def _mem_kv(mem, g, wkv, gk):
    nb, m, d = mem.shape
    return pl.pallas_call(
        _mem_kv_kernel,
        grid=(nb,),
        in_specs=[pl.BlockSpec((1, m, d), lambda b: (b, 0, 0)),
                  pl.BlockSpec((1, d), lambda b: (0, 0)),
                  pl.BlockSpec((d, 2 * d), lambda b: (0, 0)),
                  pl.BlockSpec((1, X_HEAD_DIM), lambda b: (0, 0))],
        out_specs=[pl.BlockSpec((1, m, d), lambda b: (b, 0, 0)), pl.BlockSpec((1, m, d), lambda b: (b, 0, 0))],
        out_shape=[jax.ShapeDtypeStruct((nb, m, d), BF16), jax.ShapeDtypeStruct((nb, m, d), BF16)],
        compiler_params=_cparams(1),
    )(mem, g, wkv, gk)


def _cross_kernel(sid_ref, x_ref, g_ref, wq_ref, gq_ref, k_ref, v_ref, wo_ref, o_ref):
    del sid_ref
    x = x_ref[...]
    u = _rms(x, g_ref[...]).astype(BF16)
    q = _dot(u, wq_ref[...])
    scale = X_HEAD_DIM ** -0.5
    outs = []
    for h in range(X_HEADS):
        sl = slice(h * X_HEAD_DIM, (h + 1) * X_HEAD_DIM)
        qh = (_rms(q[:, sl], gq_ref[...]) * scale).astype(BF16)
        s = _dot_nt(qh, k_ref[0, :, sl])
        m = jnp.max(s, axis=-1, keepdims=True)
        e = jnp.exp(s - m)
        p = (e / jnp.sum(e, axis=-1, keepdims=True)).astype(BF16)
        outs.append(_dot(p, v_ref[0, :, sl]).astype(BF16))
    o = jnp.concatenate(outs, axis=1)
    o_ref[...] = x + _dot(o, wo_ref[...])


def _cross(x, g, wq, gq, k, v, wo, groups, tm=512):
    t, d = x.shape
    _, _, sid = _seq_table(groups, tm)
    m = k.shape[1]

    def full(shape):
        return pl.BlockSpec(shape, lambda i, s: (0, 0))

    grid_spec = pltpu.PrefetchScalarGridSpec(
        num_scalar_prefetch=1,
        grid=(t // tm,),
        in_specs=[pl.BlockSpec((tm, d), lambda i, s: (i, 0)), full((1, d)), full((d, d)), full((1, X_HEAD_DIM)),
                  pl.BlockSpec((1, m, d), lambda i, s: (s[i], 0, 0)),
                  pl.BlockSpec((1, m, d), lambda i, s: (s[i], 0, 0)),
                  full((d, d))],
        out_specs=pl.BlockSpec((tm, d), lambda i, s: (i, 0)),
    )
    return pl.pallas_call(
        _cross_kernel,
        grid_spec=grid_spec,
        out_shape=jax.ShapeDtypeStruct((t, d), F32),
        compiler_params=_cparams(1),
    )(jnp.asarray(sid), x, g, wq, gq, k, v, wo)


def _ffn_kernel(first_ref, last_ref, x_ref, xp_ref, xn_ref, g_ref, wup_ref, cw_ref, cb_ref, wdn_ref, o_ref, u_sc,
                *, tm):
    i = pl.program_id(0)
    x = x_ref[...]
    g = g_ref[...]
    u_sc[0:tm, :] = _rms(x, g).astype(BF16)
    u_sc[tm:tm + 8, :] = _rms(xp_ref[...], g).astype(BF16)
    u_sc[tm + 8:tm + 16, :] = _rms(xn_ref[...], g).astype(BF16)
    has_prev = (first_ref[i] == 0).astype(F32)
    has_next = (last_ref[i] == 0).astype(F32)
    row = lax.broadcasted_iota(jnp.int32, (tm, FF_CHUNK), 0)
    first = row == 0
    last = row == tm - 1
    u = u_sc[...]

    def up(col0):
        return _dot(u, wup_ref[:, col0:col0 + FF_CHUNK])

    def conv(h, col0):
        hm = h[:tm]
        h_prev = jnp.where(first, h[tm + 7:tm + 8] * has_prev, pltpu.roll(hm, 1, 0))
        h_next = jnp.where(last, h[tm + 8:tm + 9] * has_next, pltpu.roll(hm, tm - 1, 0))
        w = cw_ref[:, col0:col0 + FF_CHUNK]
        return h_prev * w[0:1] + hm * w[1:2] + h_next * w[2:3] + cb_ref[:, col0:col0 + FF_CHUNK]

    n_chunks = D_FF // FF_CHUNK
    acc = x
    h_next = (up(0), up(D_FF))
    for c in range(n_chunks):
        h_a, h_g = h_next
        if c + 1 < n_chunks:
            h_next = (up((c + 1) * FF_CHUNK), up(D_FF + (c + 1) * FF_CHUNK))
        a = conv(h_a, c * FF_CHUNK)
        gg = conv(h_g, D_FF + c * FF_CHUNK)
        act = (a * (gg * _sigmoid(gg))).astype(BF16)
        acc = acc + _dot(act, wdn_ref[c * FF_CHUNK:(c + 1) * FF_CHUNK, :])
    o_ref[...] = acc


def _ffn(x, g, wup, cw, cb, wdn, groups, tm=512):
    t, d = x.shape
    lo, hi, _ = _seq_table(groups, tm)
    start = np.arange(t // tm) * tm
    is_first = jnp.asarray((start == lo).astype(np.int32))
    is_last = jnp.asarray((start + tm == hi).astype(np.int32))
    r8 = tm // 8
    n8 = t // 8

    def full(shape):
        return pl.BlockSpec(shape, lambda i, a, b: (0, 0))

    grid_spec = pltpu.PrefetchScalarGridSpec(
        num_scalar_prefetch=2,
        grid=(t // tm,),
        in_specs=[pl.BlockSpec((tm, d), lambda i, a, b: (i, 0)),
                  pl.BlockSpec((8, d), lambda i, a, b: (jnp.maximum(i * r8 - 1, 0), 0)),
                  pl.BlockSpec((8, d), lambda i, a, b: (jnp.minimum((i + 1) * r8, n8 - 1), 0)),
                  full((1, d)), full((d, 2 * D_FF)), full((3, 2 * D_FF)), full((1, 2 * D_FF)), full((D_FF, d))],
        out_specs=pl.BlockSpec((tm, d), lambda i, a, b: (i, 0)),
        scratch_shapes=[pltpu.VMEM((tm + 16, d), BF16)],
    )
    return pl.pallas_call(
        functools.partial(_ffn_kernel, tm=tm),
        grid_spec=grid_spec,
        out_shape=jax.ShapeDtypeStruct((t, d), F32),
        compiler_params=_cparams(1),
    )(is_first, is_last, x, x, x, g, wup, cw, cb, wdn)


def _permute_w_in(w):
    aq, ak, av = w[:, 0:1024], w[:, 1024:1280], w[:, 1280:1536]
    bq, bk, bv = w[:, 1536:2304], w[:, 2304:3072], w[:, 3072:3840]
    cq, ci, cff, cfb, cg = (w[:, 3840 + 1024 * n:4864 + 1024 * n] for n in range(5))
    gates = w[:, 8960:12032]
    pad = jnp.zeros((w.shape[0], 256), w.dtype)
    return jnp.concatenate([gates, bq, bk, bv, ak, av, pad, aq, cq, ci, cff, cfb, cg], axis=1).astype(BF16)


def _lower_bound(raw, layer):
    p = jax.nn.softmax(raw.astype(F32), axis=0)
    return (jnp.cumsum(p, axis=0) - p[0])[layer][None, :]


def _trunk(x, mem, groups, p):
    depth = p['g_mix'].shape[0]
    smax = max(s for _, s in groups)
    tabs_a, tabs_b = _rope_tables(smax)
    for layer in range(depth):
        row = lambda name: p[name][layer][None, :].astype(F32)
        wbf = lambda name: p[name][layer].astype(BF16)
        proj = _norm_proj(x, row('g_mix'), _permute_w_in(p['w_in'][layer]))
        gbq = jnp.broadcast_to(p['b_gq'][layer][:, None, :], (3, B_HEADS, B_HEAD_DIM)).reshape(1, 768)
        gbk = jnp.broadcast_to(p['b_gk'][layer][:, None, :], (3, B_HEADS, B_HEAD_DIM)).reshape(1, 768)
        aq, ak, av, bq, bk = _qk_prep(proj, tabs_a, tabs_b, row('a_gq'), row('a_gk'),
                                      gbq.astype(F32), gbk.astype(F32), groups)
        oa = _attention_a(aq, ak, av, groups)
        obs, lses = [], []
        for grp, (_, dil) in enumerate(B_PATTERNS):
            o, lse = _band_attention(bq, bk, proj, grp, dil, groups)
            obs.append(o)
            lses.append(lse)
        ocf, ocb = _hgrn(proj, _lower_bound(p['c_lb_fwd'], layer), _lower_bound(p['c_lb_bwd'], layer), groups)
        x = _merge(x, oa, obs, lses, ocf, ocb, proj, row('c_gnorm'),
                   wbf('w_br_a'), wbf('w_br_b'), wbf('w_br_c'), wbf('w_mix_out'))
        mk, mv = _mem_kv(mem, row('g_mem'), wbf('x_wkv'), row('x_gk'))
        x = _cross(x, row('g_cross'), wbf('x_wq'), row('x_gq'), mk, mv, wbf('x_wo'), groups)
        x = _ffn(x, row('g_ffn'), wbf('f_wup'), p['f_conv'][layer].astype(F32), row('f_conv_b'),
                 wbf('f_wdown'), groups)
    return x


def kernel(x_prompt, x_sample, mem_prompt, mem_sample, g_mix, w_in, a_gq, a_gk, b_gq, b_gk, c_lb_fwd, c_lb_bwd, c_gnorm, w_br_a, w_br_b, w_br_c, w_mix_out, g_cross, g_mem, x_wq, x_wkv, x_gq, x_gk, x_wo, g_ffn, f_wup, f_conv, f_conv_b, f_wdown):
    p = dict(g_mix=g_mix, w_in=w_in, a_gq=a_gq, a_gk=a_gk, b_gq=b_gq, b_gk=b_gk,
             c_lb_fwd=c_lb_fwd, c_lb_bwd=c_lb_bwd, c_gnorm=c_gnorm, w_br_a=w_br_a, w_br_b=w_br_b,
             w_br_c=w_br_c, w_mix_out=w_mix_out, g_cross=g_cross, g_mem=g_mem, x_wq=x_wq,
             x_wkv=x_wkv, x_gq=x_gq, x_gk=x_gk, x_wo=x_wo, g_ffn=g_ffn, f_wup=f_wup,
             f_conv=f_conv, f_conv_b=f_conv_b, f_wdown=f_wdown)
    groups = [(x_prompt.shape[0], x_prompt.shape[1]), (x_sample.shape[0], x_sample.shape[1])]
    d = x_prompt.shape[-1]
    x = jnp.concatenate([x_prompt.reshape(-1, d), x_sample.reshape(-1, d)], axis=0)
    mem = jnp.concatenate([mem_prompt, mem_sample], axis=0)
    y = _trunk(x, mem, groups, p)
    n_prompt = x_prompt.shape[0] * x_prompt.shape[1]
    return (y[:n_prompt].reshape(x_prompt.shape), y[n_prompt:].reshape(x_sample.shape))
```
